```python
import functools
import jax, jax.numpy as jnp
from jax import lax
import numpy as np

D_MODEL = 1024
BATCH = 2
SEQ = 8192
DEPTH = 2
DEC_BATCH = 128
DEC_SEQ = 8
PAST_LEN = 16384
PAGE_SIZE = 128

F32 = jnp.float32
H_A = 8
N_A = 64
A_WIDTH = H_A * N_A
W_LORA = 64
A_LORA = 64
G_LORA = 128
A_COLS = 3 * A_WIDTH + W_LORA + A_LORA + G_LORA
RWKV_GN_EPS = 64e-5
H_B = 8
Q_LORA = 256
KV_LORA = 128
NOPE_B = 64
ROPE_B = 32
DV_B = 64
KV_DIM = KV_LORA + ROPE_B
B_COLS = Q_LORA + KV_LORA + ROPE_B
MLA_SCALE = (NOPE_B + ROPE_B) ** -0.5
Q_BLOCK = 128
H_C = 4
DK_C = 128
DV_C = 256
C_COLS = 2 * H_C * DK_C + 2 * H_C * DV_C
RET_CHUNK = 128
RET_GN_EPS = 1e-5
N_BRANCH = 3
GATE_COLS = N_BRANCH * D_MODEL
IN_COLS = A_COLS + B_COLS + C_COLS + GATE_COLS
ROPE_THETA = 10000.0
N_EXPERTS = 16
N_GROUPS = 4
E_PER_GROUP = N_EXPERTS // N_GROUPS
TOP_K = 2
D_FF_E = 256
D_FF_SHARED = 256
PLE_DIM = 256
DN_ALPHA = (2 * DEPTH) ** 0.25
DN_BETA = (8 * DEPTH) ** -0.25
LN_EPS = 1e-5
RMS_EPS = 1e-6

kernel_name = 'rwkv7_mla_retention_gated_moe_step'


def layer_norm(x, g, b, out_dtype):
    xf = x.astype(F32)
    mu = jnp.mean(xf, -1, keepdims=True)
    var = jnp.mean(jnp.square(xf - mu), -1, keepdims=True)
    return ((xf - mu) * lax.rsqrt(var + LN_EPS) * g + b).astype(out_dtype)


def rms_norm(x, g):
    xf = x.astype(F32)
    return (xf * lax.rsqrt(jnp.mean(xf * xf, -1, keepdims=True) + RMS_EPS) * g).astype(x.dtype)


def head_norm(y, g, b, eps):
    B, T, H, d = y.shape
    yf = y.astype(F32)
    mu = jnp.mean(yf, -1, keepdims=True)
    var = jnp.mean(jnp.square(yf - mu), -1, keepdims=True)
    return ((yf - mu) * lax.rsqrt(var + eps)).reshape(B, T, H * d) * g + b


def rope(x, pos):
    half = x.shape[-1] // 2
    inv = ROPE_THETA ** (-jnp.arange(half, dtype=F32) / half)
    ang = pos.astype(F32)[:, None] * inv[None, :]
    cos = jnp.cos(ang)[None, :, None, :]
    sin = jnp.sin(ang)[None, :, None, :]
    x1 = x[..., :half].astype(F32)
    x2 = x[..., half:].astype(F32)
    return jnp.concatenate([x1 * cos - x2 * sin, x1 * sin + x2 * cos], -1).astype(x.dtype)


def wkv_scan(r, w, k, v, kk, a, S0):
    def step(S, inp):
        r_t, w_t, k_t, v_t, kk_t, a_t = inp
        sa = jnp.einsum('bhvk,bhk->bhv', S, -kk_t)
        S = S * w_t[:, :, None, :] + sa[..., None] * (kk_t * a_t)[:, :, None, :] + v_t[..., None] * k_t[:, :, None, :]
        y = jnp.einsum('bhvk,bhk->bhv', S, r_t)
        return S, y
    xs = tuple(t.transpose(1, 0, 2, 3) for t in (r, w, k, v, kk, a))
    S, ys = lax.scan(step, S0, xs)
    return ys.transpose(1, 0, 2, 3), S


def rwkv_branch(pa, shift0, S0, mu, w0, w2, a0, a2, g2, k_k, k_a, r_k, lnx_g, lnx_b, wo):
    B, T, _ = pa.shape
    paf = pa.astype(F32)
    prev = jnp.concatenate([shift0.astype(F32)[:, None], paf[:, :-1]], axis=1)
    xm = paf + (prev - paf) * mu
    o1 = A_WIDTH; o2 = 2 * A_WIDTH; o3 = 3 * A_WIDTH; o4 = o3 + W_LORA; o5 = o4 + A_LORA
    r, k, v = xm[..., :o1], xm[..., o1:o2], xm[..., o2:o3]
    wl, al, gl = xm[..., o3:o4], xm[..., o4:o5], xm[..., o5:]
    w = -jax.nn.softplus(-(w0 + jnp.tanh(wl) @ w2)) - 0.5
    decay = jnp.exp(-jnp.exp(w))
    a = jax.nn.sigmoid(a0 + al @ a2)
    g = jax.nn.sigmoid(gl) @ g2
    heads = lambda t: t.reshape(B, T, H_A, N_A)
    kk = heads(k * k_k)
    kk = kk / jnp.maximum(jnp.sqrt(jnp.sum(kk * kk, -1, keepdims=True)), 1e-12)
    k = k * (1.0 + (a - 1.0) * k_a)
    rh, kh, vh = heads(r), heads(k), heads(v)
    y, S = wkv_scan(rh, heads(decay), kh, vh, kk, heads(a), S0.astype(F32))
    y = head_norm(y, lnx_g, lnx_b, RWKV_GN_EPS)
    bonus = (jnp.sum(rh * kh * r_k, -1, keepdims=True) * vh).reshape(B, T, A_WIDTH)
    return ((y + bonus) * g) @ wo, S, pa[:, -1]


def mla_attend_prompt(q, kv):
    B, S, H, L = q.shape
    nb = S // Q_BLOCK
    qb = q.reshape(B, nb, Q_BLOCK, H, L).transpose(1, 0, 2, 3, 4)
    starts = jnp.arange(nb, dtype=jnp.int32) * Q_BLOCK
    kpos = jnp.arange(S, dtype=jnp.int32)

    def block(args):
        qblk, start = args
        s = jnp.einsum('bqhl,bkl->bhqk', qblk, kv).astype(F32) * MLA_SCALE
        qpos = start + jnp.arange(Q_BLOCK, dtype=jnp.int32)
        s = jnp.where(kpos[None, :] <= qpos[:, None], s, -jnp.inf)
        p = jax.nn.softmax(s, axis=-1).astype(kv.dtype)
        return jnp.einsum('bhqk,bkl->bqhl', p, kv)[..., :KV_LORA]

    o = lax.map(block, (qb, starts))
    return o.transpose(1, 0, 2, 3, 4).reshape(B, S, H, KV_LORA)


def mla_attend_sample(q, kv, kv_past):
    T = q.shape[1]
    P = kv_past.shape[1]
    s_past = jnp.einsum('bthl,bpl->bhtp', q, kv_past).astype(F32)
    s_new = jnp.einsum('bthl,bul->bhtu', q, kv).astype(F32)
    causal = jnp.arange(T)[None, :] <= jnp.arange(T)[:, None]
    s_new = jnp.where(causal, s_new, -jnp.inf)
    p = jax.nn.softmax(jnp.concatenate([s_past, s_new], -1) * MLA_SCALE, axis=-1)
    o = (jnp.einsum('bhtp,bpl->bthl', p[..., :P].astype(kv_past.dtype), kv_past)
         + jnp.einsum('bhtu,bul->bthl', p[..., P:].astype(kv.dtype), kv))
    return o[..., :KV_LORA]


def mla_branch(pb, pos, attend, q_norm_g, w_qb, kv_norm_g, w_uk, w_uv, wo):
    B, T, _ = pb.shape
    cq = rms_norm(pb[..., :Q_LORA], q_norm_g)
    q = (cq @ w_qb).reshape(B, T, H_B, NOPE_B + ROPE_B)
    q_pe = rope(q[..., NOPE_B:], pos)
    ckv = rms_norm(pb[..., Q_LORA:Q_LORA + KV_LORA], kv_norm_g)
    k_pe = rope(pb[..., Q_LORA + KV_LORA:][:, :, None, :], pos)[:, :, 0]
    kv_row = jnp.concatenate([ckv, k_pe.astype(ckv.dtype)], -1)
    q_lat = jnp.einsum('bthn,rhn->bthr', q[..., :NOPE_B], w_uk)
    q_cat = jnp.concatenate([q_lat, q_pe.astype(q_lat.dtype)], -1)
    o_lat = attend(q_cat, kv_row)
    o = jnp.einsum('bthr,rhv->bthv', o_lat, w_uv).reshape(B, T, H_B * DV_B)
    return o @ wo, kv_row


def ret_log_gamma():
    return jnp.log1p(-jnp.exp2(-5.0 - jnp.arange(H_C, dtype=F32)))


def retention_chunk(q, k, v, R, lg):
    C = q.shape[1]
    idx = jnp.arange(C, dtype=F32)
    diff = idx[:, None] - idx[None, :]
    dmat = jnp.where(diff >= 0, jnp.exp(lg[:, None, None] * jnp.maximum(diff, 0.0)), 0.0)
    att = jnp.einsum('bihd,bjhd->bhij', q, k) * dmat
    o = jnp.einsum('bhij,bjhe->bihe', att, v)
    o = o + jnp.einsum('bihd,bhde->bihe', q, R) * jnp.exp((idx[:, None] + 1.0) * lg[None, :])[None, :, :, None]
    kdec = k * jnp.exp((C - 1.0 - idx)[:, None] * lg[None, :])[None, :, :, None]
    R = R * jnp.exp(C * lg)[None, :, None, None] + jnp.einsum('bjhd,bjhe->bhde', kdec, v)
    return o, R


def retention_scan(q, k, v, R0):
    B, T, H, _ = q.shape
    c = min(RET_CHUNK, T)
    nc = T // c
    lg = ret_log_gamma()
    blk = lambda t: t.reshape(B, nc, c, H, t.shape[-1]).transpose(1, 0, 2, 3, 4)

    def step(R, inp):
        qc, kc, vc = inp
        o, R = retention_chunk(qc, kc, vc, R, lg)
        return R, o

    R, o = lax.scan(step, R0, (blk(q), blk(k), blk(v)))
    return o.transpose(1, 0, 2, 3, 4).reshape(B, T, H, DV_C), R


def retention_branch(pc, pos, R0, gn_g, gn_b, wo):
    B, T, _ = pc.shape
    pcf = pc.astype(F32)
    qk = H_C * DK_C
    vw = H_C * DV_C
    q = rope(pcf[..., :qk].reshape(B, T, H_C, DK_C), pos)
    k = rope(pcf[..., qk:2 * qk].reshape(B, T, H_C, DK_C), pos) * DK_C ** -0.5
    v = pcf[..., 2 * qk:2 * qk + vw].reshape(B, T, H_C, DV_C)
    g = pcf[..., 2 * qk + vw:]
    o, R = retention_scan(q, k, v, R0.astype(F32))
    o = head_norm(o, gn_g, gn_b, RET_GN_EPS) * jax.nn.silu(g)
    return o @ wo, R


def moe(h, router_w, router_b, we_gate, we_up, we_down, ws_gate, ws_up, ws_down):
    B, T, _ = h.shape
    score = jax.nn.sigmoid(jnp.einsum('btd,de->bte', h, router_w).astype(F32))
    sel = score + router_b.astype(F32)
    grp_score = lax.top_k(sel.reshape(B, T, N_GROUPS, E_PER_GROUP), 2)[0].sum(-1)
    grp = jnp.argmax(grp_score, axis=-1)
    expert_grp = jnp.arange(N_EXPERTS) // E_PER_GROUP
    masked = jnp.where(expert_grp == grp[..., None], sel, -jnp.inf)
    _, eidx = lax.top_k(masked, TOP_K)
    w = jnp.take_along_axis(score, eidx, axis=-1)
    w = w / jnp.sum(w, -1, keepdims=True)
    gates = jnp.sum(jax.nn.one_hot(eidx, N_EXPERTS, dtype=F32) * w[..., None], axis=-2)
    hg = jnp.einsum('btd,edf->btef', h, we_gate)
    hu = jnp.einsum('btd,edf->btef', h, we_up)
    act = jax.nn.silu(hg) * hu * gates[..., None].astype(h.dtype)
    routed = jnp.einsum('btef,efd->btd', act, we_down)
    shared = (jax.nn.silu(h @ ws_gate) * (h @ ws_up)) @ ws_down
    return routed + shared


def trunk_layer(h, p, pos, shift0, wkv0, ret0, attend, lw, router_w, router_b):
    B, T, _ = h.shape
    proj = jnp.einsum('btd,dc->btc', h, lw['w_in'])
    o0 = A_COLS; o1 = o0 + B_COLS; o2 = o1 + C_COLS
    ya, wkv, shift = rwkv_branch(proj[..., :o0], shift0, wkv0, lw['mu_a'], lw['w0'], lw['w2'], lw['a0'], lw['a2'],
                                 lw['g2'], lw['k_k'], lw['k_a'], lw['r_k'], lw['lnx_g'], lw['lnx_b'], lw['wo_a'])
    yb, kv_row = mla_branch(proj[..., o0:o1], pos, attend, lw['q_norm_g'], lw['w_qb'], lw['kv_norm_g'],
                            lw['w_uk'], lw['w_uv'], lw['wo_b'])
    yc, ret = retention_branch(proj[..., o1:o2], pos, ret0, lw['gn_c_g'], lw['gn_c_b'], lw['wo_c'])
    gate = jax.nn.sigmoid(proj[..., o2:].astype(F32)).reshape(B, T, N_BRANCH, D_MODEL)
    merged = gate[..., 0, :] * ya + gate[..., 1, :] * yb + gate[..., 2, :] * yc
    mix = merged.astype(h.dtype) @ lw['w_out']
    h = layer_norm(DN_ALPHA * h + mix, lw['ln1_g'], lw['ln1_b'], h.dtype)
    ple = jax.nn.sigmoid(h @ lw['w_ple_gate']) * (p @ lw['w_ple'])
    ffn = moe(h, router_w, router_b, lw['we_gate'], lw['we_up'], lw['we_down'], lw['ws_gate'], lw['ws_up'], lw['ws_down'])
    h = layer_norm(DN_ALPHA * h + ffn + ple, lw['ln2_g'], lw['ln2_b'], h.dtype)
    return h, kv_row, wkv, shift, ret


def setup_inputs(seed: int = 0) -> dict:
    key = jax.random.key(seed)
    ks = iter(jax.random.split(key, 64))
    nrm = lambda shape, scale: jax.random.normal(next(ks), shape, F32) * scale
    n_pages = PAST_LEN // PAGE_SIZE
    n_used = DEC_BATCH * n_pages
    n_phys = n_used + max(1, n_used // 4)
    page_table = jax.random.permutation(next(ks), n_phys)[:n_used].astype(jnp.int32).reshape(DEC_BATCH, n_pages)
    D = D_MODEL
    return dict(
        x_prompt=nrm((BATCH, SEQ, D), 1.0),
        x_sample=nrm((DEC_BATCH, DEC_SEQ, D), 1.0),
        cache_mla=nrm((DEPTH, n_phys, PAGE_SIZE, KV_DIM), 1.0),
        state_wkv=nrm((DEPTH, DEC_BATCH, H_A, N_A, N_A), 0.3),
        state_shift=nrm((DEPTH, DEC_BATCH, A_COLS), 1.0),
        state_ret=nrm((DEPTH, DEC_BATCH, H_C, DK_C, DV_C), 1.0),
        page_table=page_table,
        p_prompt=nrm((DEPTH, BATCH, SEQ, PLE_DIM), 1.0),
        p_sample=nrm((DEPTH, DEC_BATCH, DEC_SEQ, PLE_DIM), 1.0),
        ln_emb_g=1.0 + nrm((D,), 0.05),
        ln_emb_b=nrm((D,), 0.01),
        w_in=nrm((DEPTH, D, IN_COLS), D ** -0.5),
        mu_a=jax.random.uniform(next(ks), (DEPTH, A_COLS), F32),
        w0=nrm((DEPTH, A_WIDTH), 0.5),
        w2=nrm((DEPTH, W_LORA, A_WIDTH), 0.5 * W_LORA ** -0.5),
        a0=nrm((DEPTH, A_WIDTH), 0.1),
        a2=nrm((DEPTH, A_LORA, A_WIDTH), 0.5 * A_LORA ** -0.5),
        g2=nrm((DEPTH, G_LORA, A_WIDTH), G_LORA ** -0.5),
        k_k=0.85 + nrm((DEPTH, A_WIDTH), 0.05),
        k_a=1.0 + nrm((DEPTH, A_WIDTH), 0.05),
        r_k=nrm((DEPTH, H_A, N_A), 0.1),
        lnx_g=1.0 + nrm((DEPTH, A_WIDTH), 0.05),
        lnx_b=nrm((DEPTH, A_WIDTH), 0.01),
        wo_a=nrm((DEPTH, A_WIDTH, D), DN_BETA * A_WIDTH ** -0.5),
        q_norm_g=1.0 + nrm((DEPTH, Q_LORA), 0.05),
        w_qb=nrm((DEPTH, Q_LORA, H_B * (NOPE_B + ROPE_B)), Q_LORA ** -0.5),
        kv_norm_g=1.0 + nrm((DEPTH, KV_LORA), 0.05),
        w_uk=nrm((DEPTH, KV_LORA, H_B, NOPE_B), KV_LORA ** -0.5),
        w_uv=nrm((DEPTH, KV_LORA, H_B, DV_B), KV_LORA ** -0.5),
        wo_b=nrm((DEPTH, H_B * DV_B, D), DN_BETA * (H_B * DV_B) ** -0.5),
        gn_c_g=1.0 + nrm((DEPTH, H_C * DV_C), 0.05),
        gn_c_b=nrm((DEPTH, H_C * DV_C), 0.01),
        wo_c=nrm((DEPTH, H_C * DV_C, D), DN_BETA * (H_C * DV_C) ** -0.5),
        w_out=nrm((DEPTH, D, D), DN_BETA * D ** -0.5),
        ln1_g=1.0 + nrm((DEPTH, D), 0.05),
        ln1_b=nrm((DEPTH, D), 0.01),
        router_w=nrm((D, N_EXPERTS), D ** -0.5),
        router_b=nrm((N_EXPERTS,), 0.01),
        we_gate=nrm((DEPTH, N_EXPERTS, D, D_FF_E), D ** -0.5),
        we_up=nrm((DEPTH, N_EXPERTS, D, D_FF_E), D ** -0.5),
        we_down=nrm((DEPTH, N_EXPERTS, D_FF_E, D), DN_BETA * D_FF_E ** -0.5),
        ws_gate=nrm((DEPTH, D, D_FF_SHARED), D ** -0.5),
        ws_up=nrm((DEPTH, D, D_FF_SHARED), D ** -0.5),
        ws_down=nrm((DEPTH, D_FF_SHARED, D), DN_BETA * D_FF_SHARED ** -0.5),
        w_ple_gate=nrm((DEPTH, D, D), D ** -0.5),
        w_ple=nrm((DEPTH, PLE_DIM, D), DN_BETA * PLE_DIM ** -0.5),
        ln2_g=1.0 + nrm((DEPTH, D), 0.05),
        ln2_b=nrm((DEPTH, D), 0.01),
    )


def reference(x_prompt, x_sample, cache_mla, state_wkv, state_shift, state_ret, page_table, p_prompt, p_sample,
              ln_emb_g, ln_emb_b, w_in, mu_a, w0, w2, a0, a2, g2, k_k, k_a, r_k, lnx_g, lnx_b, wo_a,
              q_norm_g, w_qb, kv_norm_g, w_uk, w_uv, wo_b, gn_c_g, gn_c_b, wo_c, w_out, ln1_g, ln1_b,
              router_w, router_b, we_gate, we_up, we_down, ws_gate, ws_up, ws_down, w_ple_gate, w_ple, ln2_g, ln2_b):
    Bp, Sp, _ = x_prompt.shape
    Bs, Ts, _ = x_sample.shape
    pos_p = jnp.arange(Sp, dtype=jnp.int32)
    pos_s = PAST_LEN + jnp.arange(Ts, dtype=jnp.int32)
    hp = layer_norm(x_prompt, ln_emb_g, ln_emb_b, x_prompt.dtype)
    hs = layer_norm(x_sample, ln_emb_g, ln_emb_b, x_sample.dtype)
    zero_shift = jnp.zeros((Bp, A_COLS), F32)
    zero_wkv = jnp.zeros((Bp, H_A, N_A, N_A), F32)
    zero_ret = jnp.zeros((Bp, H_C, DK_C, DV_C), F32)
    kvp_l, kvs_l, wkvp_l, wkvs_l, shp_l, shs_l, retp_l, rets_l = [], [], [], [], [], [], [], []
    for i in range(DEPTH):
        lw = dict(w_in=w_in[i], mu_a=mu_a[i], w0=w0[i], w2=w2[i], a0=a0[i], a2=a2[i], g2=g2[i], k_k=k_k[i],
                  k_a=k_a[i], r_k=r_k[i], lnx_g=lnx_g[i], lnx_b=lnx_b[i], wo_a=wo_a[i], q_norm_g=q_norm_g[i],
                  w_qb=w_qb[i], kv_norm_g=kv_norm_g[i], w_uk=w_uk[i], w_uv=w_uv[i], wo_b=wo_b[i],
                  gn_c_g=gn_c_g[i], gn_c_b=gn_c_b[i], wo_c=wo_c[i], w_out=w_out[i], ln1_g=ln1_g[i], ln1_b=ln1_b[i],
                  we_gate=we_gate[i], we_up=we_up[i], we_down=we_down[i], ws_gate=ws_gate[i], ws_up=ws_up[i],
                  ws_down=ws_down[i], w_ple_gate=w_ple_gate[i], w_ple=w_ple[i], ln2_g=ln2_g[i], ln2_b=ln2_b[i])
        hp, kv_p, wkv_p, sh_p, ret_p = trunk_layer(hp, p_prompt[i], pos_p, zero_shift, zero_wkv, zero_ret,
                                                   mla_attend_prompt, lw, router_w, router_b)
        past = cache_mla[i][page_table].reshape(Bs, -1, KV_DIM)
        hs, kv_s, wkv_s, sh_s, ret_s = trunk_layer(hs, p_sample[i], pos_s, state_shift[i], state_wkv[i], state_ret[i],
                                                   functools.partial(mla_attend_sample, kv_past=past), lw,
                                                   router_w, router_b)
        kvp_l.append(kv_p); kvs_l.append(kv_s)
        wkvp_l.append(wkv_p); wkvs_l.append(wkv_s)
        shp_l.append(sh_p); shs_l.append(sh_s)
        retp_l.append(ret_p); rets_l.append(ret_s)
    return (hp, hs, jnp.stack(kvp_l), jnp.stack(kvs_l), jnp.stack(wkvp_l), jnp.stack(wkvs_l),
            jnp.stack(shp_l), jnp.stack(shs_l), jnp.stack(retp_l), jnp.stack(rets_l))
```

```python
import functools
import math

import jax
import jax.numpy as jnp
from jax import lax
from jax.experimental import pallas as pl
from jax.experimental.pallas import tpu as pltpu

F32 = jnp.float32
BF16 = jnp.bfloat16

D_MODEL = 1024
H_A, N_A = 8, 64
A_WIDTH = H_A * N_A
W_LORA, A_LORA, G_LORA = 64, 64, 128
A_COLS = 3 * A_WIDTH + W_LORA + A_LORA + G_LORA
RWKV_GN_EPS = 64e-5
H_B, Q_LORA, KV_LORA, NOPE_B, ROPE_B, DV_B = 8, 256, 128, 64, 32, 64
KV_DIM = KV_LORA + ROPE_B
B_COLS = Q_LORA + KV_LORA + ROPE_B
B_PAD = 640
MLA_SCALE = (NOPE_B + ROPE_B) ** -0.5
H_C, DK_C, DV_C = 4, 128, 256
C_COLS = 2 * H_C * DK_C + 2 * H_C * DV_C
RET_CHUNK = 128
RET_GN_EPS = 1e-5
RET_LOG_GAMMA = tuple(math.log1p(-(2.0 ** (-5.0 - h))) for h in range(H_C))
N_BRANCH = 3
GATE_COLS = N_BRANCH * D_MODEL
ROPE_THETA = 10000.0
N_EXPERTS, N_GROUPS, E_PER_GROUP, D_FF_E, D_FF_SHARED = 16, 4, 4, 256, 256
PLE_DIM = 256
PAGE_SIZE = 128
DEPTH = 2
DN_ALPHA = (2 * DEPTH) ** 0.25
LN_EPS = 1e-5
RMS_EPS = 1e-6

VMEM_LIMIT = 56 * 1024 * 1024


def _pow2_tile(n, pref):
    t = 1
    while t * 2 <= pref and n % (t * 2) == 0:
        t *= 2
    return t


def _params(sem):
    return pltpu.CompilerParams(dimension_semantics=sem, vmem_limit_bytes=VMEM_LIMIT)


def _const_spec(shape):
    nd = len(shape)
    return pl.BlockSpec(shape, lambda *_: (0,) * nd, pipeline_mode=pl.Buffered(1))


def _sigmoid(x):
    return 1.0 / (1.0 + jnp.exp(-x))


def _ln_rows(x, g, b, eps):
    mu = jnp.mean(x, -1, keepdims=True)
    xc = x - mu
    var = jnp.mean(xc * xc, -1, keepdims=True)
    return xc * lax.rsqrt(var + eps) * g + b


def _dot(a, b):
    return jnp.dot(a, b, preferred_element_type=F32)


def _dot_nt(a, b):
    return lax.dot_general(a, b, (((1,), (1,)), ((), ())), preferred_element_type=F32)


def _dot_tn(a, b):
    return lax.dot_general(a, b, (((0,), (0,)), ((), ())), preferred_element_type=F32)


def _ln_kernel(x_ref, g_ref, b_ref, o_ref):
    o_ref[...] = _ln_rows(x_ref[...], g_ref[...], b_ref[...], LN_EPS)


def _layer_norm(x, g, b):
    n, d = x.shape
    tm = _pow2_tile(n, 512)
    return pl.pallas_call(
        _ln_kernel,
        grid=(n // tm,),
        in_specs=[pl.BlockSpec((tm, d), lambda i: (i, 0)), _const_spec((1, d)), _const_spec((1, d))],
        out_specs=pl.BlockSpec((tm, d), lambda i: (i, 0)),
        out_shape=jax.ShapeDtypeStruct((n, d), F32),
        compiler_params=_params(("parallel",)),
        name="embed_ln",
    )(x, g.reshape(1, d), b.reshape(1, d))


def _proj_kernel(h_ref, wa_ref, wb_ref, wc_ref, pa_ref, pb_ref, pc_ref):
    x = h_ref[...].astype(BF16)
    pa_ref[...] = _dot(x, wa_ref[...])
    pb_ref[...] = _dot(x, wb_ref[...])
    pc_ref[...] = _dot(x, wc_ref[...])


def _proj(h, wa, wb, wc):
    n, d = h.shape
    tm = _pow2_tile(n, 256)
    row = lambda w: pl.BlockSpec((tm, w), lambda i: (i, 0))
    return pl.pallas_call(
        _proj_kernel,
        grid=(n // tm,),
        in_specs=[row(d), _const_spec(wa.shape), _const_spec(wb.shape), _const_spec(wc.shape)],
        out_specs=[row(A_COLS), row(B_PAD), row(C_COLS)],
        out_shape=[jax.ShapeDtypeStruct((n, A_COLS), F32), jax.ShapeDtypeStruct((n, B_PAD), F32),
                   jax.ShapeDtypeStruct((n, C_COLS), F32)],
        compiler_params=_params(("parallel",)),
        name="in_proj",
    )(h, wa, wb, wc)


def _head_ones(n, dtype):
    r = lax.broadcasted_iota(jnp.int32, (n, n), 0) // N_A
    c = lax.broadcasted_iota(jnp.int32, (n, n), 1) // N_A
    return (r == c).astype(dtype)


def _wkv_kernel(pa_ref, sh0_ref, s0_ref, mu_ref, w0_ref, a0_ref, wwa_ref, g2_ref, kk_ref, ka_ref, rk_ref,
                lng_ref, lnb_ref, ya_ref, sout_ref,
                prev_scr, prevm_scr, kk_s, wr_s, dec_s, nkka_s, k2_s, v_s, c1_s, c2_s, y_s, *, Bb, Tc, G):
    c = pl.program_id(1)
    rows = Bb * Tc

    @pl.when(c == 0)
    def _():
        prev_scr[...] = sh0_ref[...]
        sout_ref[...] = s0_ref[...]

    pa = pa_ref[...].reshape(rows, A_COLS)
    prevm_scr[...] = pltpu.roll(pa, 1, 0)
    for bb in range(Bb):
        prevm_scr[bb * Tc:bb * Tc + 1, :] = prev_scr[bb:bb + 1, :]
    for bb in range(Bb):
        prev_scr[bb:bb + 1, :] = pa[(bb + 1) * Tc - 1:(bb + 1) * Tc, :]
    xm = pa + (prevm_scr[...] - pa) * mu_ref[...]

    o1, o2, o3 = A_WIDTH, 2 * A_WIDTH, 3 * A_WIDTH
    r = xm[:, :o1]
    k = xm[:, o1:o2]
    v = xm[:, o2:o3]
    wa = xm[:, o3:o3 + W_LORA + A_LORA]
    gl = xm[:, o3 + W_LORA + A_LORA:]
    lane = lax.broadcasted_iota(jnp.int32, wa.shape, 1)
    wa = jnp.where(lane < W_LORA, jnp.tanh(wa), wa)
    lo = _dot(wa.astype(BF16), wwa_ref[...])
    nz = -(w0_ref[...] + lo[:, :A_WIDTH])
    softplus = jnp.maximum(nz, 0.0) + jnp.log(1.0 + jnp.exp(-jnp.abs(nz)))
    decay = jnp.exp(-jnp.exp(-softplus - 0.5))
    a = _sigmoid(a0_ref[...] + lo[:, A_WIDTH:])
    g = _dot(_sigmoid(gl).astype(BF16), g2_ref[...])

    ones_f32 = _head_ones(A_WIDTH, F32)
    segsum = lambda x: jnp.dot(x, ones_f32, precision=lax.Precision.HIGHEST, preferred_element_type=F32)
    kk = k * kk_ref[...]
    kk = kk / jnp.maximum(jnp.sqrt(segsum(kk * kk)), 1e-12)
    k2 = k * (1.0 + (a - 1.0) * ka_ref[...])
    nkka = -(kk * a)
    bonus = segsum(r * k2 * rk_ref[...]) * v

    kk_s[...] = kk
    wr_s[...] = decay * r
    dec_s[...] = decay
    nkka_s[...] = nkka
    k2_s[...] = k2
    v_s[...] = v
    c1_s[...] = segsum(nkka * r)
    c2_s[...] = segsum(k2 * r)

    half = A_WIDTH // 2
    ones_bf = _head_ones(half, BF16)
    eye = (lax.broadcasted_iota(jnp.int32, (N_A, A_WIDTH), 0)
           == (lax.broadcasted_iota(jnp.int32, (N_A, A_WIDTH), 1) % N_A))

    def head_sums(x):
        xb = x.astype(BF16)
        return jnp.concatenate([_dot(xb[:, :half], ones_bf), _dot(xb[:, half:], ones_bf)], axis=1)

    for grp in range(Bb // G):
        def step(t, states):
            lhs = []
            for gi in range(G):
                row = (grp * G + gi) * Tc + t
                s = states[gi]
                lhs += [s * kk_s[pl.ds(row, 1), :], s * wr_s[pl.ds(row, 1), :],
                        jnp.where(eye, v_s[pl.ds(row, 1), :], 0.0)]
            red = head_sums(jnp.concatenate(lhs, axis=0))
            new = []
            for gi in range(G):
                row = (grp * G + gi) * Tc + t
                base = gi * 3 * N_A
                sa = red[base:base + N_A]
                yb = red[base + N_A:base + 2 * N_A]
                vc = red[base + 2 * N_A:base + 3 * N_A]
                new.append(states[gi] * dec_s[pl.ds(row, 1), :] + sa * nkka_s[pl.ds(row, 1), :]
                           + vc * k2_s[pl.ds(row, 1), :])
                yfull = yb + sa * c1_s[pl.ds(row, 1), :] + vc * c2_s[pl.ds(row, 1), :]
                y_s[pl.ds(row, 1), :] = jnp.sum(jnp.where(eye, yfull, 0.0), axis=0, keepdims=True)
            return tuple(new)

        init = tuple(sout_ref[grp * G + gi] for gi in range(G))
        fin = lax.fori_loop(0, Tc, step, init)
        for gi in range(G):
            sout_ref[grp * G + gi] = fin[gi]

    y = y_s[...]
    mu = segsum(y) * (1.0 / N_A)
    yc = y - mu
    var = segsum(yc * yc) * (1.0 / N_A)
    yn = yc * lax.rsqrt(var + RWKV_GN_EPS) * lng_ref[...] + lnb_ref[...]
    ya_ref[...] = ((yn + bonus) * g).reshape(Bb, Tc, A_WIDTH).astype(ya_ref.dtype)


def _wkv(pa, shift0, s0, lw, Bb, Tc, out_dtype):
    B, T, _ = pa.shape
    rows = Bb * Tc
    G = 2 if Bb % 2 == 0 else 1
    vec = lambda: _const_spec((1, A_WIDTH))
    scr = lambda: pltpu.VMEM((rows, A_WIDTH), F32)
    return pl.pallas_call(
        functools.partial(_wkv_kernel, Bb=Bb, Tc=Tc, G=G),
        grid=(B // Bb, T // Tc),
        in_specs=[pl.BlockSpec((Bb, Tc, A_COLS), lambda b, c: (b, c, 0)),
                  pl.BlockSpec((Bb, A_COLS), lambda b, c: (b, 0)),
                  pl.BlockSpec((Bb, N_A, A_WIDTH), lambda b, c: (b, 0, 0)),
                  _const_spec((1, A_COLS)), vec(), vec(),
                  _const_spec((W_LORA + A_LORA, 2 * A_WIDTH)), _const_spec((G_LORA, A_WIDTH)),
                  vec(), vec(), vec(), vec(), vec()],
        out_specs=[pl.BlockSpec((Bb, Tc, A_WIDTH), lambda b, c: (b, c, 0)),
                   pl.BlockSpec((Bb, N_A, A_WIDTH), lambda b, c: (b, 0, 0))],
        out_shape=[jax.ShapeDtypeStruct((B, T, A_WIDTH), out_dtype),
                   jax.ShapeDtypeStruct((B, N_A, A_WIDTH), F32)],
        scratch_shapes=[pltpu.VMEM((Bb, A_COLS), F32), pltpu.VMEM((rows, A_COLS), F32)] + [scr() for _ in range(9)],
        compiler_params=_params(("parallel", "arbitrary")),
        name="rwkv_mix",
    )(pa, shift0, s0, lw["mu_a"], lw["w0"], lw["a0"], lw["w_wa"], lw["g2"], lw["k_k"], lw["k_a"], lw["r_k"],
      lw["lnx_g"], lw["lnx_b"])


def _mla_pre_kernel(pb_ref, cos_ref, sin_ref, qg_ref, kvg_ref, wqn_ref, wqpe_ref, wqps_ref, wuk_ref,
                    qcat_ref, kv_ref, kvb_ref):
    pb = pb_ref[...]
    xq = pb[:, :Q_LORA]
    cq = xq * lax.rsqrt(jnp.mean(xq * xq, -1, keepdims=True) + RMS_EPS) * qg_ref[...]
    cqb = cq.astype(BF16)
    qn = _dot(cqb, wqn_ref[...])
    qlat = _dot(qn.astype(BF16), wuk_ref[...])
    cos = cos_ref[...]
    sin = sin_ref[...]
    for h in range(H_B):
        qpe = _dot(cqb, wqpe_ref[h]) * cos + _dot(cqb, wqps_ref[h]) * sin
        qcat_ref[h, :, :KV_LORA] = qlat[:, h * KV_LORA:(h + 1) * KV_LORA].astype(BF16)
        qcat_ref[h, :, KV_LORA:] = qpe.astype(BF16)
    xkv = pb[:, Q_LORA:Q_LORA + KV_LORA]
    ckv = xkv * lax.rsqrt(jnp.mean(xkv * xkv, -1, keepdims=True) + RMS_EPS) * kvg_ref[...]
    kpe = pb[:, Q_LORA + KV_LORA:B_COLS] * cos + pb[:, 512:512 + ROPE_B] * sin
    kv_ref[:, :KV_LORA] = ckv
    kv_ref[:, KV_LORA:] = kpe
    kvb_ref[:, :KV_LORA] = ckv.astype(BF16)
    kvb_ref[:, KV_LORA:] = kpe.astype(BF16)


def _mla_pre(pb, cos, sin, lw):
    n = pb.shape[0]
    tm = _pow2_tile(n, 256)
    row = lambda w: pl.BlockSpec((tm, w), lambda i: (i, 0))
    return pl.pallas_call(
        _mla_pre_kernel,
        grid=(n // tm,),
        in_specs=[row(B_PAD), row(ROPE_B), row(ROPE_B), _const_spec((1, Q_LORA)), _const_spec((1, KV_LORA)),
                  _const_spec((Q_LORA, H_B * NOPE_B)), _const_spec((H_B, Q_LORA, ROPE_B)),
                  _const_spec((H_B, Q_LORA, ROPE_B)), _const_spec((H_B * NOPE_B, H_B * KV_LORA))],
        out_specs=[pl.BlockSpec((H_B, tm, KV_DIM), lambda i: (0, i, 0)), row(KV_DIM), row(KV_DIM)],
        out_shape=[jax.ShapeDtypeStruct((H_B, n, KV_DIM), BF16), jax.ShapeDtypeStruct((n, KV_DIM), F32),
                   jax.ShapeDtypeStruct((n, KV_DIM), BF16)],
        compiler_params=_params(("parallel",)),
        name="mla_pre",
    )(pb, cos, sin, lw["q_norm_g"], lw["kv_norm_g"], lw["w_qn"], lw["w_qpe"], lw["w_qpe_sw"], lw["w_uk_bd"])


def _softmax_step(carry, s, vals):
    m, l, acc = carry
    m_new = jnp.maximum(m, jnp.max(s, axis=-1, keepdims=True))
    p = jnp.exp(s - m_new)
    alpha = jnp.exp(m - m_new)
    l = alpha * l + jnp.sum(p, axis=-1, keepdims=True)
    acc = alpha * acc + _dot(p.astype(vals.dtype), vals)
    return m_new, l, acc


def _flash_kernel(q_ref, kv_ref, o_ref, *, BQ, BK):
    i = pl.program_id(1)
    rows = H_B * BQ
    q = q_ref[...].reshape(rows, KV_DIM)
    nfull = (i * BQ) // BK

    def block(j, carry, masked):
        kvj = kv_ref[pl.ds(pl.multiple_of(j * BK, BK), BK), :]
        s = _dot_nt(q, kvj) * MLA_SCALE
        if masked:
            qpos = i * BQ + (lax.broadcasted_iota(jnp.int32, (rows, BK), 0) % BQ)
            kpos = j * BK + lax.broadcasted_iota(jnp.int32, (rows, BK), 1)
            s = jnp.where(kpos <= qpos, s, -jnp.inf)
        return _softmax_step(carry, s, kvj[:, :KV_LORA])

    init = (jnp.full((rows, 1), -jnp.inf, F32), jnp.zeros((rows, 1), F32), jnp.zeros((rows, KV_LORA), F32))
    carry = lax.fori_loop(0, nfull, lambda j, c: block(j, c, False), init)
    m, l, acc = block(nfull, carry, True)
    o = acc / l
    for h in range(H_B):
        o_ref[:, h * KV_LORA:(h + 1) * KV_LORA] = o[h * BQ:(h + 1) * BQ]


def _flash_prompt(qcat, kvb, B, T):
    BQ = min(128, T)
    BK = min(512, T)
    nq = T // BQ
    return pl.pallas_call(
        functools.partial(_flash_kernel, BQ=BQ, BK=BK),
        grid=(B, nq),
        in_specs=[pl.BlockSpec((H_B, BQ, KV_DIM), lambda b, i: (0, b * nq + i, 0)),
                  pl.BlockSpec((T, KV_DIM), lambda b, i: (b, 0))],
        out_specs=pl.BlockSpec((BQ, H_B * KV_LORA), lambda b, i: (b * nq + i, 0)),
        out_shape=jax.ShapeDtypeStruct((B * T, H_B * KV_LORA), F32),
        compiler_params=_params(("parallel", "arbitrary")),
        name="mla_prompt_attn",
    )(qcat, kvb)


def _sample_attn_kernel(pt_ref, q_ref, kvn_ref, cache_ref, o_ref, buf, sem, *, layer, NB, Ts, PG, nchunks):
    rows = H_B * Ts
    for bi in range(NB):
        q32 = q_ref[:, bi * Ts:(bi + 1) * Ts, :].reshape(rows, KV_DIM)
        q = q32.astype(BF16)

        def page_copy(page, slot, p):
            return pltpu.make_async_copy(cache_ref.at[layer, page], buf.at[slot, p], sem.at[slot])

        def start(c, slot):
            for p in range(PG):
                page_copy(pt_ref[0, bi, c * PG + p], slot, p).start()

        def wait(slot):
            for p in range(PG):
                page_copy(0, slot, p).wait()

        start(0, 0)

        def body(c, carry):
            slot = lax.rem(c, 2)

            @pl.when(c + 1 < nchunks)
            def _():
                start(c + 1, 1 - slot)

            wait(slot)
            kvc = buf[slot].reshape(PG * PAGE_SIZE, KV_DIM).astype(BF16)
            s = _dot_nt(q, kvc) * MLA_SCALE
            return _softmax_step(carry, s, kvc[:, :KV_LORA])

        init = (jnp.full((rows, 1), -jnp.inf, F32), jnp.zeros((rows, 1), F32), jnp.zeros((rows, KV_LORA), F32))
        carry = lax.fori_loop(0, nchunks, body, init)
        kvn = kvn_ref[bi * Ts:(bi + 1) * Ts, :]
        s = _dot_nt(q32, kvn) * MLA_SCALE
        tq = lax.broadcasted_iota(jnp.int32, (rows, Ts), 0) % Ts
        tk = lax.broadcasted_iota(jnp.int32, (rows, Ts), 1)
        s = jnp.where(tk <= tq, s, -jnp.inf)
        m, l, acc = _softmax_step(carry, s, kvn[:, :KV_LORA])
        o = acc / l
        for h in range(H_B):
            o_ref[bi * Ts:(bi + 1) * Ts, h * KV_LORA:(h + 1) * KV_LORA] = o[h * Ts:(h + 1) * Ts]


def _sample_attn(qcat32, kv_new, cache, page_table, layer, B, Ts):
    npages = page_table.shape[1]
    NB = 2 if B % 2 == 0 else 1
    PG = _pow2_tile(npages, 16)
    pt = page_table.reshape(B // NB, NB, npages)
    return pl.pallas_call(
        functools.partial(_sample_attn_kernel, layer=layer, NB=NB, Ts=Ts, PG=PG, nchunks=npages // PG),
        grid=(B // NB,),
        in_specs=[pl.BlockSpec((1, NB, npages), lambda g: (g, 0, 0), memory_space=pltpu.SMEM),
                  pl.BlockSpec((H_B, NB * Ts, KV_DIM), lambda g: (0, g, 0)),
                  pl.BlockSpec((NB * Ts, KV_DIM), lambda g: (g, 0)),
                  pl.BlockSpec(memory_space=pl.ANY)],
        out_specs=pl.BlockSpec((NB * Ts, H_B * KV_LORA), lambda g: (g, 0)),
        out_shape=jax.ShapeDtypeStruct((B * Ts, H_B * KV_LORA), F32),
        scratch_shapes=[pltpu.VMEM((2, PG, PAGE_SIZE, KV_DIM), F32), pltpu.SemaphoreType.DMA((2,))],
        compiler_params=_params(("arbitrary",)),
        name="mla_sample_attn",
    )(pt, qcat32, kv_new, cache)


def _ret_kernel(pc_ref, cos_ref, sin_ref, r0_ref, gng_ref, gnb_ref, yc_ref, rout_ref, *, Bb, C):
    c = pl.program_id(1)

    @pl.when(c == 0)
    def _():
        rout_ref[...] = r0_ref[...]

    mm = BF16 if C >= 16 else F32
    pc = pc_ref[...]
    cos = cos_ref[...]
    sin = sin_ref[...]
    qk = H_C * DK_C
    vw = H_C * DV_C
    diff = (lax.broadcasted_iota(jnp.int32, (C, C), 0) - lax.broadcasted_iota(jnp.int32, (C, C), 1)).astype(F32)
    row_v = lax.broadcasted_iota(jnp.int32, (C, DV_C), 0).astype(F32)
    row_k = lax.broadcasted_iota(jnp.int32, (C, DK_C), 0).astype(F32)
    for h in range(H_C):
        lg = RET_LOG_GAMMA[h]
        dmat = jnp.where(diff >= 0, jnp.exp(lg * jnp.maximum(diff, 0.0)), 0.0)
        row_decay = jnp.exp((row_v + 1.0) * lg)
        k_decay = jnp.exp((C - 1.0 - row_k) * lg)
        chunk_decay = math.exp(C * lg)
        qh = pc[:, h * DK_C:(h + 1) * DK_C]
        kh = pc[:, qk + h * DK_C:qk + (h + 1) * DK_C]
        q = qh * cos + pltpu.roll(qh, DK_C // 2, 1) * sin
        k = (kh * cos + pltpu.roll(kh, DK_C // 2, 1) * sin) * DK_C ** -0.5
        v = pc[:, 2 * qk + h * DV_C:2 * qk + (h + 1) * DV_C]
        g = pc[:, 2 * qk + vw + h * DV_C:2 * qk + vw + (h + 1) * DV_C]
        for bb in range(Bb):
            sl = slice(bb * C, (bb + 1) * C)
            qb = q[sl].astype(mm)
            kb = k[sl]
            vb = v[sl].astype(mm)
            att = _dot_nt(qb, kb.astype(mm)) * dmat
            state = rout_ref[bb, h]
            o = _dot(att.astype(mm), vb) + _dot(qb, state.astype(mm)) * row_decay
            rout_ref[bb, h] = state * chunk_decay + _dot_tn((kb * k_decay).astype(mm), vb)
            mu = jnp.mean(o, -1, keepdims=True)
            oc = o - mu
            var = jnp.mean(oc * oc, -1, keepdims=True)
            on = (oc * lax.rsqrt(var + RET_GN_EPS) * gng_ref[:, h * DV_C:(h + 1) * DV_C]
                  + gnb_ref[:, h * DV_C:(h + 1) * DV_C])
            gg = g[sl]
            yc_ref[sl, h * DV_C:(h + 1) * DV_C] = on * (gg * _sigmoid(gg))


def _retention(pc, cos, sin, r0, lw, B, T, Bb):
    C = min(RET_CHUNK, T)
    nc = T // C
    rows = Bb * C
    row = lambda w: pl.BlockSpec((rows, w), lambda b, c: (b * nc + c, 0))
    st = pl.BlockSpec((Bb, H_C, DK_C, DV_C), lambda b, c: (b, 0, 0, 0))
    return pl.pallas_call(
        functools.partial(_ret_kernel, Bb=Bb, C=C),
        grid=(B // Bb, nc),
        in_specs=[row(C_COLS), row(DK_C), row(DK_C), st, _const_spec((1, H_C * DV_C)), _const_spec((1, H_C * DV_C))],
        out_specs=[row(H_C * DV_C), st],
        out_shape=[jax.ShapeDtypeStruct((B * T, H_C * DV_C), F32), jax.ShapeDtypeStruct(r0.shape, F32)],
        compiler_params=_params(("parallel", "arbitrary")),
        name="retention",
    )(pc, cos, sin, r0, lw["gn_c_g"], lw["gn_c_b"])


def _merge_kernel(h_ref, ya_ref, ol_ref, yc_ref, wg_ref, woa_ref, wuv_ref, wob_ref, woc_ref, wout_ref,
                  g_ref, b_ref, o_ref):
    h = h_ref[...]
    gate = _sigmoid(_dot(h.astype(BF16), wg_ref[...]))
    ya = _dot(ya_ref[...].astype(BF16), woa_ref[...])
    ob = _dot(ol_ref[...].astype(BF16), wuv_ref[...])
    yb = _dot(ob.astype(BF16), wob_ref[...])
    yc = _dot(yc_ref[...].astype(BF16), woc_ref[...])
    d = D_MODEL
    merged = gate[:, :d] * ya + gate[:, d:2 * d] * yb + gate[:, 2 * d:] * yc
    mix = _dot(merged.astype(BF16), wout_ref[...])
    o_ref[...] = _ln_rows(DN_ALPHA * h + mix, g_ref[...], b_ref[...], LN_EPS)


def _merge(h, ya, olat, yc, lw):
    n, d = h.shape
    tm = _pow2_tile(n, 256)
    row = lambda w: pl.BlockSpec((tm, w), lambda i: (i, 0))
    ws = [lw["w_gate"], lw["wo_a"], lw["w_uv_bd"], lw["wo_b"], lw["wo_c"], lw["w_out"], lw["ln1_g"], lw["ln1_b"]]
    return pl.pallas_call(
        _merge_kernel,
        grid=(n // tm,),
        in_specs=[row(d), row(A_WIDTH), row(H_B * KV_LORA), row(H_C * DV_C)] + [_const_spec(w.shape) for w in ws],
        out_specs=row(d),
        out_shape=jax.ShapeDtypeStruct((n, d), F32),
        compiler_params=_params(("parallel",)),
        name="merge_ln1",
    )(h, ya, olat, yc, *ws)


def _top2_sum(a, b, c, d):
    m_ab, n_ab = jnp.maximum(a, b), jnp.minimum(a, b)
    m_cd, n_cd = jnp.maximum(c, d), jnp.minimum(c, d)
    return jnp.maximum(m_ab, m_cd) + jnp.maximum(jnp.minimum(m_ab, m_cd), jnp.maximum(n_ab, n_cd))


def _ffn_kernel(h_ref, p_ref, rwt_ref, rb_ref, wpg_ref, wp_ref, weg_ref, weu_ref, wed_ref, wsg_ref, wsu_ref,
                wsd_ref, exp_ref, g_ref, b_ref, o_ref):
    h = h_ref[...]
    hb = h.astype(BF16)
    tm = h.shape[0]
    ple = _sigmoid(_dot(hb, wpg_ref[...])) * _dot(p_ref[...].astype(BF16), wp_ref[...])

    logits = lax.dot_general(rwt_ref[...], h, (((1,), (1,)), ((), ())), precision=lax.Precision.HIGHEST,
                             preferred_element_type=F32)
    score = _sigmoid(logits)
    sel = score + rb_ref[...]
    srow = [sel[e:e + 1] for e in range(N_EXPERTS)]
    grp_score = [_top2_sum(*srow[E_PER_GROUP * g:E_PER_GROUP * (g + 1)]) for g in range(N_GROUPS)]
    best = grp_score[0]
    gidx = jnp.zeros((1, tm), jnp.int32)
    for g in range(1, N_GROUPS):
        upd = grp_score[g] > best
        gidx = jnp.where(upd, g, gidx)
        best = jnp.where(upd, grp_score[g], best)
    wrow = []
    for e in range(N_EXPERTS):
        g = e // E_PER_GROUP
        rank = jnp.zeros((1, tm), jnp.int32)
        for e2 in range(E_PER_GROUP * g, E_PER_GROUP * (g + 1)):
            if e2 < e:
                rank += (srow[e2] >= srow[e]).astype(jnp.int32)
            elif e2 > e:
                rank += (srow[e2] > srow[e]).astype(jnp.int32)
        chosen = (gidx == g) & (rank < 2)
        wrow.append(jnp.where(chosen, score[e:e + 1], 0.0))
    wsum = wrow[0]
    for e in range(1, N_EXPERTS):
        wsum = wsum + wrow[e]
    gates = jnp.concatenate(wrow, axis=0) / wsum
    gt = gates.T
    g_hi = gt.astype(BF16)
    g_lo = (gt - g_hi.astype(F32)).astype(BF16)

    routed = jnp.zeros((tm, D_MODEL), F32)
    cw = E_PER_GROUP * D_FF_E
    for g in range(N_GROUPS):
        cs = slice(g * cw, (g + 1) * cw)
        hg = _dot(hb, weg_ref[:, cs])
        hu = _dot(hb, weu_ref[:, cs])
        gfull = _dot(g_hi, exp_ref[:, cs]) + _dot(g_lo, exp_ref[:, cs])
        act = (hg * _sigmoid(hg)) * hu * gfull
        routed = routed + _dot(act.astype(BF16), wed_ref[cs, :])
    sg = _dot(hb, wsg_ref[...])
    shared = _dot(((sg * _sigmoid(sg)) * _dot(hb, wsu_ref[...])).astype(BF16), wsd_ref[...])
    o_ref[...] = _ln_rows(DN_ALPHA * h + (routed + shared) + ple, g_ref[...], b_ref[...], LN_EPS)


def _ffn(h, p, lw, shared):
    n, d = h.shape
    tm = _pow2_tile(n, 256)
    row = lambda w: pl.BlockSpec((tm, w), lambda i: (i, 0))
    ws = [shared["router_wt"], shared["router_b"], lw["w_ple_gate"], lw["w_ple"], lw["we_gate"], lw["we_up"],
          lw["we_down"], lw["ws_gate"], lw["ws_up"], lw["ws_down"], shared["expand"], lw["ln2_g"], lw["ln2_b"]]
    return pl.pallas_call(
        _ffn_kernel,
        grid=(n // tm,),
        in_specs=[row(d), row(PLE_DIM)] + [_const_spec(w.shape) for w in ws],
        out_specs=row(d),
        out_shape=jax.ShapeDtypeStruct((n, d), F32),
        compiler_params=_params(("parallel",)),
        name="ple_moe_ln2",
    )(h, p, *ws)


def _block_diag(blocks):
    nb = len(blocks)
    r, c = blocks[0].shape
    out = jnp.zeros((nb * r, nb * c), blocks[0].dtype)
    for i, b in enumerate(blocks):
        out = out.at[i * r:(i + 1) * r, i * c:(i + 1) * c].set(b)
    return out


def _layer_weights(i, w_in, mu_a, w0, w2, a0, a2, g2, k_k, k_a, r_k, lnx_g, lnx_b, wo_a, q_norm_g, w_qb, kv_norm_g,
                   w_uk, w_uv, wo_b, gn_c_g, gn_c_b, wo_c, w_out, ln1_g, ln1_b, we_gate, we_up, we_down, ws_gate,
                   ws_up, ws_down, w_ple_gate, w_ple, ln2_g, ln2_b):
    win = w_in[i]
    o0, o1, o2 = A_COLS, A_COLS + B_COLS, A_COLS + B_COLS + C_COLS
    wb = win[:, o0:o1]
    half = ROPE_B // 2
    kpe0 = Q_LORA + KV_LORA
    wb_pad = jnp.zeros((D_MODEL, B_PAD), F32).at[:, :B_COLS].set(wb)
    wb_pad = wb_pad.at[:, 512:512 + half].set(wb[:, kpe0 + half:kpe0 + ROPE_B])
    wb_pad = wb_pad.at[:, 512 + half:512 + ROPE_B].set(wb[:, kpe0:kpe0 + half])
    zeros = jnp.zeros((W_LORA, A_WIDTH), F32)
    w_wa = jnp.concatenate([jnp.concatenate([w2[i], zeros], 1), jnp.concatenate([zeros, a2[i]], 1)], 0)
    wq = w_qb[i].reshape(Q_LORA, H_B, NOPE_B + ROPE_B)
    wq_pe = wq[:, :, NOPE_B:].transpose(1, 0, 2)
    wq_pe_sw = jnp.concatenate([wq_pe[..., half:], wq_pe[..., :half]], -1)
    vec = lambda x: x.reshape(1, -1).astype(F32)
    return dict(
        w_a=win[:, :o0].astype(BF16), w_b=wb_pad.astype(BF16), w_c=win[:, o1:o2].astype(BF16),
        w_gate=win[:, o2:].astype(BF16),
        mu_a=vec(mu_a[i]), w0=vec(w0[i]), a0=vec(a0[i]), w_wa=w_wa.astype(BF16), g2=g2[i].astype(BF16),
        k_k=vec(k_k[i]), k_a=vec(k_a[i]), r_k=vec(r_k[i]), lnx_g=vec(lnx_g[i]), lnx_b=vec(lnx_b[i]),
        wo_a=wo_a[i].astype(BF16),
        q_norm_g=vec(q_norm_g[i]), kv_norm_g=vec(kv_norm_g[i]),
        w_qn=wq[:, :, :NOPE_B].reshape(Q_LORA, H_B * NOPE_B).astype(BF16),
        w_qpe=wq_pe.astype(BF16), w_qpe_sw=wq_pe_sw.astype(BF16),
        w_uk_bd=_block_diag([w_uk[i][:, h, :].T for h in range(H_B)]).astype(BF16),
        w_uv_bd=_block_diag([w_uv[i][:, h, :] for h in range(H_B)]).astype(BF16),
        wo_b=wo_b[i].astype(BF16),
        gn_c_g=vec(gn_c_g[i]), gn_c_b=vec(gn_c_b[i]), wo_c=wo_c[i].astype(BF16),
        w_out=w_out[i].astype(BF16), ln1_g=vec(ln1_g[i]), ln1_b=vec(ln1_b[i]),
        we_gate=we_gate[i].transpose(1, 0, 2).reshape(D_MODEL, N_EXPERTS * D_FF_E).astype(BF16),
        we_up=we_up[i].transpose(1, 0, 2).reshape(D_MODEL, N_EXPERTS * D_FF_E).astype(BF16),
        we_down=we_down[i].reshape(N_EXPERTS * D_FF_E, D_MODEL).astype(BF16),
        ws_gate=ws_gate[i].astype(BF16), ws_up=ws_up[i].astype(BF16), ws_down=ws_down[i].astype(BF16),
        w_ple_gate=w_ple_gate[i].astype(BF16), w_ple=w_ple[i].astype(BF16),
        ln2_g=vec(ln2_g[i]), ln2_b=vec(ln2_b[i]),
    )


def _rope_tables(pos, half):
    inv = ROPE_THETA ** (-jnp.arange(half, dtype=F32) / half)
    ang = pos.astype(F32)[:, None] * inv[None, :]
    cos, sin = jnp.cos(ang), jnp.sin(ang)
    return jnp.concatenate([cos, cos], -1), jnp.concatenate([-sin, sin], -1)


def _wkv_to_lanes(s):
    B = s.shape[0]
    return s.transpose(0, 2, 1, 3).reshape(B, N_A, A_WIDTH)


def _wkv_from_lanes(s):
    B = s.shape[0]
    return s.reshape(B, N_A, H_A, N_A).transpose(0, 2, 1, 3)


def _trunk_layer(h, p, tabs, shift0, wkv0, ret0, attend, lw, shared, B, T, wkv_blk, ret_bb):
    pa, pb, pc = _proj(h, lw["w_a"], lw["w_b"], lw["w_c"])
    ya, wkv = _wkv(pa.reshape(B, T, A_COLS), shift0, wkv0, lw, wkv_blk[0], wkv_blk[1], F32)
    qcat, kv_row, kv_bf = _mla_pre(pb, tabs["cos_b"], tabs["sin_b"], lw)
    olat = attend(qcat, kv_row, kv_bf)
    yc, ret = _retention(pc, tabs["cos_c"], tabs["sin_c"], ret0, lw, B, T, ret_bb)
    h1 = _merge(h, ya.reshape(B * T, A_WIDTH), olat, yc, lw)
    h2 = _ffn(h1, p, lw, shared)
    shift = pa.reshape(B, T, A_COLS)[:, -1]
    return h2, kv_row.reshape(B, T, KV_DIM), wkv, shift, ret


def kernel(x_prompt, x_sample, cache_mla, state_wkv, state_shift, state_ret, page_table, p_prompt, p_sample,
           ln_emb_g, ln_emb_b, w_in, mu_a, w0, w2, a0, a2, g2, k_k, k_a, r_k, lnx_g, lnx_b, wo_a,
           q_norm_g, w_qb, kv_norm_g, w_uk, w_uv, wo_b, gn_c_g, gn_c_b, wo_c, w_out, ln1_g, ln1_b,
           router_w, router_b, we_gate, we_up, we_down, ws_gate, ws_up, ws_down, w_ple_gate, w_ple, ln2_g, ln2_b):
    Bp, Sp, D = x_prompt.shape
    Bs, Ts, _ = x_sample.shape
    past_len = page_table.shape[1] * PAGE_SIZE

    pos_p = jnp.tile(jnp.arange(Sp, dtype=jnp.int32), Bp)
    pos_s = jnp.tile(past_len + jnp.arange(Ts, dtype=jnp.int32), Bs)
    tabs = []
    for pos in (pos_p, pos_s):
        cb, sb = _rope_tables(pos, ROPE_B // 2)
        cc, sc = _rope_tables(pos, DK_C // 2)
        tabs.append(dict(cos_b=cb, sin_b=sb, cos_c=cc, sin_c=sc))
    expand = (jnp.arange(N_EXPERTS)[:, None] == (jnp.arange(N_EXPERTS * D_FF_E)[None, :] // D_FF_E)).astype(BF16)
    shared = dict(router_wt=router_w.T.astype(F32), router_b=router_b.reshape(N_EXPERTS, 1).astype(F32),
                  expand=expand)

    hp = _layer_norm(x_prompt.reshape(Bp * Sp, D), ln_emb_g, ln_emb_b)
    hs = _layer_norm(x_sample.reshape(Bs * Ts, D), ln_emb_g, ln_emb_b)
    zero_shift = jnp.zeros((Bp, A_COLS), F32)
    zero_wkv = jnp.zeros((Bp, N_A, A_WIDTH), F32)
    zero_ret = jnp.zeros((Bp, H_C, DK_C, DV_C), F32)
    wkv_tc = min(128, Sp)
    samp_bb = 8 if Bs % 8 == 0 else Bs

    outs = [[] for _ in range(8)]
    for i in range(DEPTH):
        lw = _layer_weights(i, w_in, mu_a, w0, w2, a0, a2, g2, k_k, k_a, r_k, lnx_g, lnx_b, wo_a, q_norm_g, w_qb,
                            kv_norm_g, w_uk, w_uv, wo_b, gn_c_g, gn_c_b, wo_c, w_out, ln1_g, ln1_b, we_gate, we_up,
                            we_down, ws_gate, ws_up, ws_down, w_ple_gate, w_ple, ln2_g, ln2_b)
        attend_p = lambda qcat, kv_row, kv_bf: _flash_prompt(qcat, kv_bf, Bp, Sp)
        hp, kv_p, wkv_p, sh_p, ret_p = _trunk_layer(
            hp, p_prompt[i].reshape(Bp * Sp, PLE_DIM), tabs[0], zero_shift, zero_wkv, zero_ret, attend_p, lw, shared,
            Bp, Sp, (Bp, wkv_tc), 1)
        attend_s = lambda qcat, kv_row, kv_bf, i=i: _sample_attn(qcat.astype(F32), kv_row, cache_mla, page_table, i,
                                                                 Bs, Ts)
        hs, kv_s, wkv_s, sh_s, ret_s = _trunk_layer(
            hs, p_sample[i].reshape(Bs * Ts, PLE_DIM), tabs[1], state_shift[i], _wkv_to_lanes(state_wkv[i]),
            state_ret[i], attend_s, lw, shared, Bs, Ts, (samp_bb, Ts), samp_bb)
        for lst, val in zip(outs, (kv_p, kv_s, _wkv_from_lanes(wkv_p), _wkv_from_lanes(wkv_s), sh_p, sh_s,
                                   ret_p, ret_s)):
            lst.append(val)
    return (hp.reshape(Bp, Sp, D), hs.reshape(Bs, Ts, D)) + tuple(jnp.stack(l) for l in outs)
```

```python
import functools
import math

import jax
import jax.numpy as jnp
from jax import lax
from jax.experimental import pallas as pl
from jax.experimental.pallas import tpu as pltpu

F32 = jnp.float32
BF16 = jnp.bfloat16

D_MODEL = 1024
H_A, N_A = 8, 64
A_WIDTH = H_A * N_A
W_LORA, A_LORA, G_LORA = 64, 64, 128
A_COLS = 3 * A_WIDTH + W_LORA + A_LORA + G_LORA
RWKV_GN_EPS = 64e-5
H_B, Q_LORA, KV_LORA, NOPE_B, ROPE_B, DV_B = 8, 256, 128, 64, 32, 64
KV_DIM = KV_LORA + ROPE_B
B_COLS = Q_LORA + KV_LORA + ROPE_B
B_PAD = 640
MLA_SCALE = (NOPE_B + ROPE_B) ** -0.5
H_C, DK_C, DV_C = 4, 128, 256
C_COLS = 2 * H_C * DK_C + 2 * H_C * DV_C
RET_CHUNK = 128
RET_GN_EPS = 1e-5
RET_LOG_GAMMA = tuple(math.log1p(-(2.0 ** (-5.0 - h))) for h in range(H_C))
N_BRANCH = 3
GATE_COLS = N_BRANCH * D_MODEL
ROPE_THETA = 10000.0
N_EXPERTS, N_GROUPS, E_PER_GROUP, D_FF_E, D_FF_SHARED = 16, 4, 4, 256, 256
PLE_DIM = 256
PAGE_SIZE = 128
DEPTH = 2
DN_ALPHA = (2 * DEPTH) ** 0.25
LN_EPS = 1e-5
RMS_EPS = 1e-6

VMEM_LIMIT = 56 * 1024 * 1024


def _pow2_tile(n, pref):
    t = 1
    while t * 2 <= pref and n % (t * 2) == 0:
        t *= 2
    return t


def _params(sem):
    return pltpu.CompilerParams(dimension_semantics=sem, vmem_limit_bytes=VMEM_LIMIT)


def _const_spec(shape):
    nd = len(shape)
    return pl.BlockSpec(shape, lambda *_: (0,) * nd, pipeline_mode=pl.Buffered(1))


def _sigmoid(x):
    return 1.0 / (1.0 + jnp.exp(-x))


def _ln_rows(x, g, b, eps):
    mu = jnp.mean(x, -1, keepdims=True)
    xc = x - mu
    var = jnp.mean(xc * xc, -1, keepdims=True)
    return xc * lax.rsqrt(var + eps) * g + b


def _dot(a, b):
    return jnp.dot(a, b, preferred_element_type=F32)


def _dot_nt(a, b):
    return lax.dot_general(a, b, (((1,), (1,)), ((), ())), preferred_element_type=F32)


def _dot_tn(a, b):
    return lax.dot_general(a, b, (((0,), (0,)), ((), ())), preferred_element_type=F32)


def _ln_kernel(x_ref, g_ref, b_ref, o_ref):
    o_ref[...] = _ln_rows(x_ref[...], g_ref[...], b_ref[...], LN_EPS)


def _layer_norm(x, g, b):
    n, d = x.shape
    tm = _pow2_tile(n, 512)
    return pl.pallas_call(
        _ln_kernel,
        grid=(n // tm,),
        in_specs=[pl.BlockSpec((tm, d), lambda i: (i, 0)), _const_spec((1, d)), _const_spec((1, d))],
        out_specs=pl.BlockSpec((tm, d), lambda i: (i, 0)),
        out_shape=jax.ShapeDtypeStruct((n, d), F32),
        compiler_params=_params(("parallel",)),
        name="embed_ln",
    )(x, g.reshape(1, d), b.reshape(1, d))


def _proj_kernel(h_ref, wa_ref, wb_ref, wc_ref, pa_ref, pb_ref, pc_ref):
    x = h_ref[...].astype(BF16)
    pa_ref[...] = _dot(x, wa_ref[...])
    pb_ref[...] = _dot(x, wb_ref[...])
    pc_ref[...] = _dot(x, wc_ref[...])


def _proj(h, wa, wb, wc):
    n, d = h.shape
    tm = _pow2_tile(n, 256)
    row = lambda w: pl.BlockSpec((tm, w), lambda i: (i, 0))
    return pl.pallas_call(
        _proj_kernel,
        grid=(n // tm,),
        in_specs=[row(d), _const_spec(wa.shape), _const_spec(wb.shape), _const_spec(wc.shape)],
        out_specs=[row(A_COLS), row(B_PAD), row(C_COLS)],
        out_shape=[jax.ShapeDtypeStruct((n, A_COLS), F32), jax.ShapeDtypeStruct((n, B_PAD), F32),
                   jax.ShapeDtypeStruct((n, C_COLS), F32)],
        compiler_params=_params(("parallel",)),
        name="in_proj",
    )(h, wa, wb, wc)


def _head_ones(n, dtype):
    r = lax.broadcasted_iota(jnp.int32, (n, n), 0) // N_A
    c = lax.broadcasted_iota(jnp.int32, (n, n), 1) // N_A
    return (r == c).astype(dtype)


def _wkv_kernel(pa_ref, sh0_ref, s0_ref, mu_ref, w0_ref, a0_ref, wwa_ref, g2_ref, kk_ref, ka_ref, rk_ref,
                lng_ref, lnb_ref, ya_ref, sout_ref,
                prev_scr, prevm_scr, kk_s, wr_s, dec_s, nkka_s, k2_s, v_s, c1_s, c2_s, y_s, *, Bb, Tc, G):
    c = pl.program_id(1)
    rows = Bb * Tc

    @pl.when(c == 0)
    def _():
        prev_scr[...] = sh0_ref[...]
        sout_ref[...] = s0_ref[...]

    pa = pa_ref[...].reshape(rows, A_COLS)
    prevm_scr[...] = pltpu.roll(pa, 1, 0)
    for bb in range(Bb):
        prevm_scr[bb * Tc:bb * Tc + 1, :] = prev_scr[bb:bb + 1, :]
    for bb in range(Bb):
        prev_scr[bb:bb + 1, :] = pa[(bb + 1) * Tc - 1:(bb + 1) * Tc, :]
    xm = pa + (prevm_scr[...] - pa) * mu_ref[...]

    o1, o2, o3 = A_WIDTH, 2 * A_WIDTH, 3 * A_WIDTH
    r = xm[:, :o1]
    k = xm[:, o1:o2]
    v = xm[:, o2:o3]
    wa = xm[:, o3:o3 + W_LORA + A_LORA]
    gl = xm[:, o3 + W_LORA + A_LORA:]
    lane = lax.broadcasted_iota(jnp.int32, wa.shape, 1)
    wa = jnp.where(lane < W_LORA, jnp.tanh(wa), wa)
    lo = _dot(wa.astype(BF16), wwa_ref[...])
    nz = -(w0_ref[...] + lo[:, :A_WIDTH])
    softplus = jnp.maximum(nz, 0.0) + jnp.log(1.0 + jnp.exp(-jnp.abs(nz)))
    decay = jnp.exp(-jnp.exp(-softplus - 0.5))
    a = _sigmoid(a0_ref[...] + lo[:, A_WIDTH:])
    g = _dot(_sigmoid(gl).astype(BF16), g2_ref[...])

    ones_f32 = _head_ones(A_WIDTH, F32)
    segsum = lambda x: jnp.dot(x, ones_f32, precision=lax.Precision.HIGHEST, preferred_element_type=F32)
    kk = k * kk_ref[...]
    kk = kk / jnp.maximum(jnp.sqrt(segsum(kk * kk)), 1e-12)
    k2 = k * (1.0 + (a - 1.0) * ka_ref[...])
    nkka = -(kk * a)
    bonus = segsum(r * k2 * rk_ref[...]) * v

    kk_s[...] = kk
    wr_s[...] = decay * r
    dec_s[...] = decay
    nkka_s[...] = nkka
    k2_s[...] = k2
    v_s[...] = v
    c1_s[...] = segsum(nkka * r)
    c2_s[...] = segsum(k2 * r)

    half = A_WIDTH // 2
    ones_bf = _head_ones(half, BF16)
    eye = (lax.broadcasted_iota(jnp.int32, (N_A, A_WIDTH), 0)
           == (lax.broadcasted_iota(jnp.int32, (N_A, A_WIDTH), 1) % N_A))

    def head_sums(x):
        xb = x.astype(BF16)
        return jnp.concatenate([_dot(xb[:, :half], ones_bf), _dot(xb[:, half:], ones_bf)], axis=1)

    for grp in range(Bb // G):
        def step(t, states):
            lhs = []
            for gi in range(G):
                row = (grp * G + gi) * Tc + t
                s = states[gi]
                lhs += [s * kk_s[pl.ds(row, 1), :], s * wr_s[pl.ds(row, 1), :],
                        jnp.where(eye, v_s[pl.ds(row, 1), :], 0.0)]
            red = head_sums(jnp.concatenate(lhs, axis=0))
            new = []
            for gi in range(G):
                row = (grp * G + gi) * Tc + t
                base = gi * 3 * N_A
                sa = red[base:base + N_A]
                yb = red[base + N_A:base + 2 * N_A]
                vc = red[base + 2 * N_A:base + 3 * N_A]
                new.append(states[gi] * dec_s[pl.ds(row, 1), :] + sa * nkka_s[pl.ds(row, 1), :]
                           + vc * k2_s[pl.ds(row, 1), :])
                yfull = yb + sa * c1_s[pl.ds(row, 1), :] + vc * c2_s[pl.ds(row, 1), :]
                y_s[pl.ds(row, 1), :] = jnp.sum(jnp.where(eye, yfull, 0.0), axis=0, keepdims=True)
            return tuple(new)

        init = tuple(sout_ref[grp * G + gi] for gi in range(G))
        fin = lax.fori_loop(0, Tc, step, init)
        for gi in range(G):
            sout_ref[grp * G + gi] = fin[gi]

    y = y_s[...]
    mu = segsum(y) * (1.0 / N_A)
    yc = y - mu
    var = segsum(yc * yc) * (1.0 / N_A)
    yn = yc * lax.rsqrt(var + RWKV_GN_EPS) * lng_ref[...] + lnb_ref[...]
    ya_ref[...] = ((yn + bonus) * g).reshape(Bb, Tc, A_WIDTH).astype(ya_ref.dtype)


def _wkv(pa, shift0, s0, lw, Bb, Tc, out_dtype):
    B, T, _ = pa.shape
    rows = Bb * Tc
    G = 2 if Bb % 2 == 0 else 1
    vec = lambda: _const_spec((1, A_WIDTH))
    scr = lambda: pltpu.VMEM((rows, A_WIDTH), F32)
    return pl.pallas_call(
        functools.partial(_wkv_kernel, Bb=Bb, Tc=Tc, G=G),
        grid=(B // Bb, T // Tc),
        in_specs=[pl.BlockSpec((Bb, Tc, A_COLS), lambda b, c: (b, c, 0)),
                  pl.BlockSpec((Bb, A_COLS), lambda b, c: (b, 0)),
                  pl.BlockSpec((Bb, N_A, A_WIDTH), lambda b, c: (b, 0, 0)),
                  _const_spec((1, A_COLS)), vec(), vec(),
                  _const_spec((W_LORA + A_LORA, 2 * A_WIDTH)), _const_spec((G_LORA, A_WIDTH)),
                  vec(), vec(), vec(), vec(), vec()],
        out_specs=[pl.BlockSpec((Bb, Tc, A_WIDTH), lambda b, c: (b, c, 0)),
                   pl.BlockSpec((Bb, N_A, A_WIDTH), lambda b, c: (b, 0, 0))],
        out_shape=[jax.ShapeDtypeStruct((B, T, A_WIDTH), out_dtype),
                   jax.ShapeDtypeStruct((B, N_A, A_WIDTH), F32)],
        scratch_shapes=[pltpu.VMEM((Bb, A_COLS), F32), pltpu.VMEM((rows, A_COLS), F32)] + [scr() for _ in range(9)],
        compiler_params=_params(("parallel", "arbitrary")),
        name="rwkv_mix",
    )(pa, shift0, s0, lw["mu_a"], lw["w0"], lw["a0"], lw["w_wa"], lw["g2"], lw["k_k"], lw["k_a"], lw["r_k"],
      lw["lnx_g"], lw["lnx_b"])


def _mla_pre_kernel(pb_ref, cos_ref, sin_ref, qg_ref, kvg_ref, wqn_ref, wqpe_ref, wqps_ref, wuk_ref,
                    qcat_ref, kv_ref, kvb_ref):
    pb = pb_ref[...]
    xq = pb[:, :Q_LORA]
    cq = xq * lax.rsqrt(jnp.mean(xq * xq, -1, keepdims=True) + RMS_EPS) * qg_ref[...]
    cqb = cq.astype(BF16)
    qn = _dot(cqb, wqn_ref[...])
    qlat = _dot(qn.astype(BF16), wuk_ref[...])
    cos = cos_ref[...]
    sin = sin_ref[...]
    for h in range(H_B):
        qpe = _dot(cqb, wqpe_ref[h]) * cos + _dot(cqb, wqps_ref[h]) * sin
        qcat_ref[h, :, :KV_LORA] = qlat[:, h * KV_LORA:(h + 1) * KV_LORA].astype(BF16)
        qcat_ref[h, :, KV_LORA:] = qpe.astype(BF16)
    xkv = pb[:, Q_LORA:Q_LORA + KV_LORA]
    ckv = xkv * lax.rsqrt(jnp.mean(xkv * xkv, -1, keepdims=True) + RMS_EPS) * kvg_ref[...]
    kpe = pb[:, Q_LORA + KV_LORA:B_COLS] * cos + pb[:, 512:512 + ROPE_B] * sin
    kv_ref[:, :KV_LORA] = ckv
    kv_ref[:, KV_LORA:] = kpe
    kvb_ref[:, :KV_LORA] = ckv.astype(BF16)
    kvb_ref[:, KV_LORA:] = kpe.astype(BF16)


def _mla_pre(pb, cos, sin, lw):
    n = pb.shape[0]
    tm = _pow2_tile(n, 256)
    row = lambda w: pl.BlockSpec((tm, w), lambda i: (i, 0))
    return pl.pallas_call(
        _mla_pre_kernel,
        grid=(n // tm,),
        in_specs=[row(B_PAD), row(ROPE_B), row(ROPE_B), _const_spec((1, Q_LORA)), _const_spec((1, KV_LORA)),
                  _const_spec((Q_LORA, H_B * NOPE_B)), _const_spec((H_B, Q_LORA, ROPE_B)),
                  _const_spec((H_B, Q_LORA, ROPE_B)), _const_spec((H_B * NOPE_B, H_B * KV_LORA))],
        out_specs=[pl.BlockSpec((H_B, tm, KV_DIM), lambda i: (0, i, 0)), row(KV_DIM), row(KV_DIM)],
        out_shape=[jax.ShapeDtypeStruct((H_B, n, KV_DIM), BF16), jax.ShapeDtypeStruct((n, KV_DIM), F32),
                   jax.ShapeDtypeStruct((n, KV_DIM), BF16)],
        compiler_params=_params(("parallel",)),
        name="mla_pre",
    )(pb, cos, sin, lw["q_norm_g"], lw["kv_norm_g"], lw["w_qn"], lw["w_qpe"], lw["w_qpe_sw"], lw["w_uk_bd"])


def _softmax_step(carry, s, vals):
    m, l, acc = carry
    m_new = jnp.maximum(m, jnp.max(s, axis=-1, keepdims=True))
    p = jnp.exp(s - m_new)
    alpha = jnp.exp(m - m_new)
    l = alpha * l + jnp.sum(p, axis=-1, keepdims=True)
    acc = alpha * acc + _dot(p.astype(vals.dtype), vals)
    return m_new, l, acc


def _flash_kernel(q_ref, kv_ref, o_ref, *, BQ, BK):
    i = pl.program_id(1)
    rows = H_B * BQ
    q = q_ref[...].reshape(rows, KV_DIM)
    nfull = (i * BQ) // BK

    def block(j, carry, masked):
        kvj = kv_ref[pl.ds(pl.multiple_of(j * BK, BK), BK), :]
        s = _dot_nt(q, kvj) * MLA_SCALE
        if masked:
            qpos = i * BQ + (lax.broadcasted_iota(jnp.int32, (rows, BK), 0) % BQ)
            kpos = j * BK + lax.broadcasted_iota(jnp.int32, (rows, BK), 1)
            s = jnp.where(kpos <= qpos, s, -jnp.inf)
        return _softmax_step(carry, s, kvj[:, :KV_LORA])

    init = (jnp.full((rows, 1), -jnp.inf, F32), jnp.zeros((rows, 1), F32), jnp.zeros((rows, KV_LORA), F32))
    carry = lax.fori_loop(0, nfull, lambda j, c: block(j, c, False), init)
    m, l, acc = block(nfull, carry, True)
    o = acc / l
    for h in range(H_B):
        o_ref[:, h * KV_LORA:(h + 1) * KV_LORA] = o[h * BQ:(h + 1) * BQ]


def _flash_prompt(qcat, kvb, B, T):
    BQ = min(128, T)
    BK = min(512, T)
    nq = T // BQ
    return pl.pallas_call(
        functools.partial(_flash_kernel, BQ=BQ, BK=BK),
        grid=(B, nq),
        in_specs=[pl.BlockSpec((H_B, BQ, KV_DIM), lambda b, i: (0, b * nq + i, 0)),
                  pl.BlockSpec((T, KV_DIM), lambda b, i: (b, 0))],
        out_specs=pl.BlockSpec((BQ, H_B * KV_LORA), lambda b, i: (b * nq + i, 0)),
        out_shape=jax.ShapeDtypeStruct((B * T, H_B * KV_LORA), F32),
        compiler_params=_params(("parallel", "arbitrary")),
        name="mla_prompt_attn",
    )(qcat, kvb)


def _sample_attn_kernel(ptc_ref, ptn_ref, q_ref, kvn_ref, cache_ref, o_ref, buf, kvb_scr, s_scr, sem,
                        *, layer, B, Ts, PG, npages):
    b = pl.program_id(0)
    slot = lax.rem(b, 2)
    rows = H_B * Ts
    width = PG * PAGE_SIZE
    nchunks = npages // PG
    lanes = 128

    def page_copy(page, sl, p):
        return pltpu.make_async_copy(cache_ref.at[layer, page], buf.at[sl, p], sem.at[sl])

    def start(pt_ref, sl):
        for p in range(npages):
            page_copy(pt_ref[0, 0, p], sl, p).start()

    @pl.when(b == 0)
    def _():
        start(ptc_ref, 0)

    @pl.when(b + 1 < B)
    def _():
        start(ptn_ref, 1 - slot)

    for p in range(npages):
        page_copy(0, slot, p).wait()

    q32 = q_ref[...].reshape(rows, KV_DIM)
    qb = q32.astype(BF16)
    mx = jnp.full((rows, lanes), -jnp.inf, F32)
    for c in range(nchunks):
        kvt = jnp.concatenate([buf[slot, c * PG + p].astype(BF16) for p in range(PG)], axis=1)
        kvb_scr[:, c * width:(c + 1) * width] = kvt
        s = _dot(qb, kvt) * MLA_SCALE
        s_scr[:, c * width:(c + 1) * width] = s
        for j in range(width // lanes):
            mx = jnp.maximum(mx, s[:, j * lanes:(j + 1) * lanes])
    kvn = kvn_ref[...]
    s_new = _dot_nt(q32, kvn) * MLA_SCALE
    tq = lax.broadcasted_iota(jnp.int32, (rows, Ts), 0) % Ts
    tk = lax.broadcasted_iota(jnp.int32, (rows, Ts), 1)
    s_new = jnp.where(tk <= tq, s_new, -jnp.inf)
    m = jnp.maximum(jnp.max(mx, axis=-1, keepdims=True), jnp.max(s_new, axis=-1, keepdims=True))
    lsum = jnp.zeros((rows, lanes), F32)
    acc = jnp.zeros((rows, KV_LORA), F32)
    for c in range(nchunks):
        p = jnp.exp(s_scr[:, c * width:(c + 1) * width] - m)
        for j in range(width // lanes):
            lsum = lsum + p[:, j * lanes:(j + 1) * lanes]
        acc = acc + _dot_nt(p.astype(BF16), kvb_scr[:KV_LORA, c * width:(c + 1) * width])
    p_new = jnp.exp(s_new - m)
    l = jnp.sum(lsum, axis=-1, keepdims=True) + jnp.sum(p_new, axis=-1, keepdims=True)
    o = (acc + _dot(p_new, kvn[:, :KV_LORA])) / l
    for h in range(H_B):
        o_ref[:, h * KV_LORA:(h + 1) * KV_LORA] = o[h * Ts:(h + 1) * Ts]


def _sample_attn(qcat32, kv_new, cache_t, page_table, layer, B, Ts):
    npages = page_table.shape[1]
    PG = _pow2_tile(npages, 16)
    past = npages * PAGE_SIZE
    pt = page_table.reshape(B, 1, npages)
    pt_spec = lambda f: pl.BlockSpec((1, 1, npages), f, memory_space=pltpu.SMEM)
    return pl.pallas_call(
        functools.partial(_sample_attn_kernel, layer=layer, B=B, Ts=Ts, PG=PG, npages=npages),
        grid=(B,),
        in_specs=[pt_spec(lambda b: (b, 0, 0)), pt_spec(lambda b: (jnp.minimum(b + 1, B - 1), 0, 0)),
                  pl.BlockSpec((H_B, Ts, KV_DIM), lambda b: (0, b, 0)),
                  pl.BlockSpec((Ts, KV_DIM), lambda b: (b, 0)),
                  pl.BlockSpec(memory_space=pl.ANY)],
        out_specs=pl.BlockSpec((Ts, H_B * KV_LORA), lambda b: (b, 0)),
        out_shape=jax.ShapeDtypeStruct((B * Ts, H_B * KV_LORA), F32),
        scratch_shapes=[pltpu.VMEM((2, npages, KV_DIM, PAGE_SIZE), F32), pltpu.VMEM((KV_DIM, past), BF16),
                        pltpu.VMEM((H_B * Ts, past), F32), pltpu.SemaphoreType.DMA((2,))],
        compiler_params=_params(("arbitrary",)),
        name="mla_sample_attn",
    )(pt, pt, qcat32, kv_new, cache_t)


def _ret_kernel(pc_ref, cos_ref, sin_ref, r0_ref, gng_ref, gnb_ref, yc_ref, rout_ref, *, Bb, C):
    c = pl.program_id(1)

    @pl.when(c == 0)
    def _():
        rout_ref[...] = r0_ref[...]

    mm = BF16 if C >= 16 else F32
    pc = pc_ref[...]
    cos = cos_ref[...]
    sin = sin_ref[...]
    qk = H_C * DK_C
    vw = H_C * DV_C
    diff = (lax.broadcasted_iota(jnp.int32, (C, C), 0) - lax.broadcasted_iota(jnp.int32, (C, C), 1)).astype(F32)
    row_v = lax.broadcasted_iota(jnp.int32, (C, DV_C), 0).astype(F32)
    row_k = lax.broadcasted_iota(jnp.int32, (C, DK_C), 0).astype(F32)
    for h in range(H_C):
        lg = RET_LOG_GAMMA[h]
        dmat = jnp.where(diff >= 0, jnp.exp(lg * jnp.maximum(diff, 0.0)), 0.0)
        row_decay = jnp.exp((row_v + 1.0) * lg)
        k_decay = jnp.exp((C - 1.0 - row_k) * lg)
        chunk_decay = math.exp(C * lg)
        qh = pc[:, h * DK_C:(h + 1) * DK_C]
        kh = pc[:, qk + h * DK_C:qk + (h + 1) * DK_C]
        q = qh * cos + pltpu.roll(qh, DK_C // 2, 1) * sin
        k = (kh * cos + pltpu.roll(kh, DK_C // 2, 1) * sin) * DK_C ** -0.5
        v = pc[:, 2 * qk + h * DV_C:2 * qk + (h + 1) * DV_C]
        g = pc[:, 2 * qk + vw + h * DV_C:2 * qk + vw + (h + 1) * DV_C]
        for bb in range(Bb):
            sl = slice(bb * C, (bb + 1) * C)
            qb = q[sl].astype(mm)
            kb = k[sl]
            vb = v[sl].astype(mm)
            att = _dot_nt(qb, kb.astype(mm)) * dmat
            state = rout_ref[bb, h]
            o = _dot(att.astype(mm), vb) + _dot(qb, state.astype(mm)) * row_decay
            rout_ref[bb, h] = state * chunk_decay + _dot_tn((kb * k_decay).astype(mm), vb)
            mu = jnp.mean(o, -1, keepdims=True)
            oc = o - mu
            var = jnp.mean(oc * oc, -1, keepdims=True)
            on = (oc * lax.rsqrt(var + RET_GN_EPS) * gng_ref[:, h * DV_C:(h + 1) * DV_C]
                  + gnb_ref[:, h * DV_C:(h + 1) * DV_C])
            gg = g[sl]
            yc_ref[sl, h * DV_C:(h + 1) * DV_C] = on * (gg * _sigmoid(gg))


def _retention(pc, cos, sin, r0, lw, B, T, Bb):
    C = min(RET_CHUNK, T)
    nc = T // C
    rows = Bb * C
    row = lambda w: pl.BlockSpec((rows, w), lambda b, c: (b * nc + c, 0))
    st = pl.BlockSpec((Bb, H_C, DK_C, DV_C), lambda b, c: (b, 0, 0, 0))
    return pl.pallas_call(
        functools.partial(_ret_kernel, Bb=Bb, C=C),
        grid=(B // Bb, nc),
        in_specs=[row(C_COLS), row(DK_C), row(DK_C), st, _const_spec((1, H_C * DV_C)), _const_spec((1, H_C * DV_C))],
        out_specs=[row(H_C * DV_C), st],
        out_shape=[jax.ShapeDtypeStruct((B * T, H_C * DV_C), F32), jax.ShapeDtypeStruct(r0.shape, F32)],
        compiler_params=_params(("parallel", "arbitrary")),
        name="retention",
    )(pc, cos, sin, r0, lw["gn_c_g"], lw["gn_c_b"])


def _merge_kernel(h_ref, ya_ref, ol_ref, yc_ref, wg_ref, woa_ref, wuv_ref, wob_ref, woc_ref, wout_ref,
                  g_ref, b_ref, o_ref):
    h = h_ref[...]
    gate = _sigmoid(_dot(h.astype(BF16), wg_ref[...]))
    ya = _dot(ya_ref[...].astype(BF16), woa_ref[...])
    ob = _dot(ol_ref[...].astype(BF16), wuv_ref[...])
    yb = _dot(ob.astype(BF16), wob_ref[...])
    yc = _dot(yc_ref[...].astype(BF16), woc_ref[...])
    d = D_MODEL
    merged = gate[:, :d] * ya + gate[:, d:2 * d] * yb + gate[:, 2 * d:] * yc
    mix = _dot(merged.astype(BF16), wout_ref[...])
    o_ref[...] = _ln_rows(DN_ALPHA * h + mix, g_ref[...], b_ref[...], LN_EPS)


def _merge(h, ya, olat, yc, lw):
    n, d = h.shape
    tm = _pow2_tile(n, 256)
    row = lambda w: pl.BlockSpec((tm, w), lambda i: (i, 0))
    ws = [lw["w_gate"], lw["wo_a"], lw["w_uv_bd"], lw["wo_b"], lw["wo_c"], lw["w_out"], lw["ln1_g"], lw["ln1_b"]]
    return pl.pallas_call(
        _merge_kernel,
        grid=(n // tm,),
        in_specs=[row(d), row(A_WIDTH), row(H_B * KV_LORA), row(H_C * DV_C)] + [_const_spec(w.shape) for w in ws],
        out_specs=row(d),
        out_shape=jax.ShapeDtypeStruct((n, d), F32),
        compiler_params=_params(("parallel",)),
        name="merge_ln1",
    )(h, ya, olat, yc, *ws)


def _top2_sum(a, b, c, d):
    m_ab, n_ab = jnp.maximum(a, b), jnp.minimum(a, b)
    m_cd, n_cd = jnp.maximum(c, d), jnp.minimum(c, d)
    return jnp.maximum(m_ab, m_cd) + jnp.maximum(jnp.minimum(m_ab, m_cd), jnp.maximum(n_ab, n_cd))


def _ffn_kernel(h_ref, p_ref, rwt_ref, rb_ref, wpg_ref, wp_ref, weg_ref, weu_ref, wed_ref, wsg_ref, wsu_ref,
                wsd_ref, exp_ref, g_ref, b_ref, o_ref):
    h = h_ref[...]
    hb = h.astype(BF16)
    tm = h.shape[0]
    ple = _sigmoid(_dot(hb, wpg_ref[...])) * _dot(p_ref[...].astype(BF16), wp_ref[...])

    logits = lax.dot_general(rwt_ref[...], h, (((1,), (1,)), ((), ())), precision=lax.Precision.HIGHEST,
                             preferred_element_type=F32)
    score = _sigmoid(logits)
    sel = score + rb_ref[...]
    srow = [sel[e:e + 1] for e in range(N_EXPERTS)]
    grp_score = [_top2_sum(*srow[E_PER_GROUP * g:E_PER_GROUP * (g + 1)]) for g in range(N_GROUPS)]
    best = grp_score[0]
    gidx = jnp.zeros((1, tm), jnp.int32)
    for g in range(1, N_GROUPS):
        upd = grp_score[g] > best
        gidx = jnp.where(upd, g, gidx)
        best = jnp.where(upd, grp_score[g], best)
    wrow = []
    for e in range(N_EXPERTS):
        g = e // E_PER_GROUP
        rank = jnp.zeros((1, tm), jnp.int32)
        for e2 in range(E_PER_GROUP * g, E_PER_GROUP * (g + 1)):
            if e2 < e:
                rank += (srow[e2] >= srow[e]).astype(jnp.int32)
            elif e2 > e:
                rank += (srow[e2] > srow[e]).astype(jnp.int32)
        chosen = (gidx == g) & (rank < 2)
        wrow.append(jnp.where(chosen, score[e:e + 1], 0.0))
    wsum = wrow[0]
    for e in range(1, N_EXPERTS):
        wsum = wsum + wrow[e]
    gates = jnp.concatenate(wrow, axis=0) / wsum
    gt = gates.T
    g_hi = gt.astype(BF16)
    g_lo = (gt - g_hi.astype(F32)).astype(BF16)

    routed = jnp.zeros((tm, D_MODEL), F32)
    cw = E_PER_GROUP * D_FF_E
    for g in range(N_GROUPS):
        cs = slice(g * cw, (g + 1) * cw)
        hg = _dot(hb, weg_ref[:, cs])
        hu = _dot(hb, weu_ref[:, cs])
        gfull = _dot(g_hi, exp_ref[:, cs]) + _dot(g_lo, exp_ref[:, cs])
        act = (hg * _sigmoid(hg)) * hu * gfull
        routed = routed + _dot(act.astype(BF16), wed_ref[cs, :])
    sg = _dot(hb, wsg_ref[...])
    shared = _dot(((sg * _sigmoid(sg)) * _dot(hb, wsu_ref[...])).astype(BF16), wsd_ref[...])
    o_ref[...] = _ln_rows(DN_ALPHA * h + (routed + shared) + ple, g_ref[...], b_ref[...], LN_EPS)


def _ffn(h, p, lw, shared):
    n, d = h.shape
    tm = _pow2_tile(n, 256)
    row = lambda w: pl.BlockSpec((tm, w), lambda i: (i, 0))
    ws = [shared["router_wt"], shared["router_b"], lw["w_ple_gate"], lw["w_ple"], lw["we_gate"], lw["we_up"],
          lw["we_down"], lw["ws_gate"], lw["ws_up"], lw["ws_down"], shared["expand"], lw["ln2_g"], lw["ln2_b"]]
    return pl.pallas_call(
        _ffn_kernel,
        grid=(n // tm,),
        in_specs=[row(d), row(PLE_DIM)] + [_const_spec(w.shape) for w in ws],
        out_specs=row(d),
        out_shape=jax.ShapeDtypeStruct((n, d), F32),
        compiler_params=_params(("parallel",)),
        name="ple_moe_ln2",
    )(h, p, *ws)


def _block_diag(blocks):
    nb = len(blocks)
    r, c = blocks[0].shape
    out = jnp.zeros((nb * r, nb * c), blocks[0].dtype)
    for i, b in enumerate(blocks):
        out = out.at[i * r:(i + 1) * r, i * c:(i + 1) * c].set(b)
    return out


def _layer_weights(i, w_in, mu_a, w0, w2, a0, a2, g2, k_k, k_a, r_k, lnx_g, lnx_b, wo_a, q_norm_g, w_qb, kv_norm_g,
                   w_uk, w_uv, wo_b, gn_c_g, gn_c_b, wo_c, w_out, ln1_g, ln1_b, we_gate, we_up, we_down, ws_gate,
                   ws_up, ws_down, w_ple_gate, w_ple, ln2_g, ln2_b):
    win = w_in[i]
    o0, o1, o2 = A_COLS, A_COLS + B_COLS, A_COLS + B_COLS + C_COLS
    wb = win[:, o0:o1]
    half = ROPE_B // 2
    kpe0 = Q_LORA + KV_LORA
    wb_pad = jnp.zeros((D_MODEL, B_PAD), F32).at[:, :B_COLS].set(wb)
    wb_pad = wb_pad.at[:, 512:512 + half].set(wb[:, kpe0 + half:kpe0 + ROPE_B])
    wb_pad = wb_pad.at[:, 512 + half:512 + ROPE_B].set(wb[:, kpe0:kpe0 + half])
    zeros = jnp.zeros((W_LORA, A_WIDTH), F32)
    w_wa = jnp.concatenate([jnp.concatenate([w2[i], zeros], 1), jnp.concatenate([zeros, a2[i]], 1)], 0)
    wq = w_qb[i].reshape(Q_LORA, H_B, NOPE_B + ROPE_B)
    wq_pe = wq[:, :, NOPE_B:].transpose(1, 0, 2)
    wq_pe_sw = jnp.concatenate([wq_pe[..., half:], wq_pe[..., :half]], -1)
    vec = lambda x: x.reshape(1, -1).astype(F32)
    return dict(
        w_a=win[:, :o0].astype(BF16), w_b=wb_pad.astype(BF16), w_c=win[:, o1:o2].astype(BF16),
        w_gate=win[:, o2:].astype(BF16),
        mu_a=vec(mu_a[i]), w0=vec(w0[i]), a0=vec(a0[i]), w_wa=w_wa.astype(BF16), g2=g2[i].astype(BF16),
        k_k=vec(k_k[i]), k_a=vec(k_a[i]), r_k=vec(r_k[i]), lnx_g=vec(lnx_g[i]), lnx_b=vec(lnx_b[i]),
        wo_a=wo_a[i].astype(BF16),
        q_norm_g=vec(q_norm_g[i]), kv_norm_g=vec(kv_norm_g[i]),
        w_qn=wq[:, :, :NOPE_B].reshape(Q_LORA, H_B * NOPE_B).astype(BF16),
        w_qpe=wq_pe.astype(BF16), w_qpe_sw=wq_pe_sw.astype(BF16),
        w_uk_bd=_block_diag([w_uk[i][:, h, :].T for h in range(H_B)]).astype(BF16),
        w_uv_bd=_block_diag([w_uv[i][:, h, :] for h in range(H_B)]).astype(BF16),
        wo_b=wo_b[i].astype(BF16),
        gn_c_g=vec(gn_c_g[i]), gn_c_b=vec(gn_c_b[i]), wo_c=wo_c[i].astype(BF16),
        w_out=w_out[i].astype(BF16), ln1_g=vec(ln1_g[i]), ln1_b=vec(ln1_b[i]),
        we_gate=we_gate[i].transpose(1, 0, 2).reshape(D_MODEL, N_EXPERTS * D_FF_E).astype(BF16),
        we_up=we_up[i].transpose(1, 0, 2).reshape(D_MODEL, N_EXPERTS * D_FF_E).astype(BF16),
        we_down=we_down[i].reshape(N_EXPERTS * D_FF_E, D_MODEL).astype(BF16),
        ws_gate=ws_gate[i].astype(BF16), ws_up=ws_up[i].astype(BF16), ws_down=ws_down[i].astype(BF16),
        w_ple_gate=w_ple_gate[i].astype(BF16), w_ple=w_ple[i].astype(BF16),
        ln2_g=vec(ln2_g[i]), ln2_b=vec(ln2_b[i]),
    )


def _rope_tables(pos, half):
    inv = ROPE_THETA ** (-jnp.arange(half, dtype=F32) / half)
    ang = pos.astype(F32)[:, None] * inv[None, :]
    cos, sin = jnp.cos(ang), jnp.sin(ang)
    return jnp.concatenate([cos, cos], -1), jnp.concatenate([-sin, sin], -1)


def _wkv_to_lanes(s):
    B = s.shape[0]
    return s.transpose(0, 2, 1, 3).reshape(B, N_A, A_WIDTH)


def _wkv_from_lanes(s):
    B = s.shape[0]
    return s.reshape(B, N_A, H_A, N_A).transpose(0, 2, 1, 3)


def _trunk_layer(h, p, tabs, shift0, wkv0, ret0, attend, lw, shared, B, T, wkv_blk, ret_bb):
    pa, pb, pc = _proj(h, lw["w_a"], lw["w_b"], lw["w_c"])
    ya, wkv = _wkv(pa.reshape(B, T, A_COLS), shift0, wkv0, lw, wkv_blk[0], wkv_blk[1], F32)
    qcat, kv_row, kv_bf = _mla_pre(pb, tabs["cos_b"], tabs["sin_b"], lw)
    olat = attend(qcat, kv_row, kv_bf)
    yc, ret = _retention(pc, tabs["cos_c"], tabs["sin_c"], ret0, lw, B, T, ret_bb)
    h1 = _merge(h, ya.reshape(B * T, A_WIDTH), olat, yc, lw)
    h2 = _ffn(h1, p, lw, shared)
    shift = pa.reshape(B, T, A_COLS)[:, -1]
    return h2, kv_row.reshape(B, T, KV_DIM), wkv, shift, ret


def kernel(x_prompt, x_sample, cache_mla, state_wkv, state_shift, state_ret, page_table, p_prompt, p_sample,
           ln_emb_g, ln_emb_b, w_in, mu_a, w0, w2, a0, a2, g2, k_k, k_a, r_k, lnx_g, lnx_b, wo_a,
           q_norm_g, w_qb, kv_norm_g, w_uk, w_uv, wo_b, gn_c_g, gn_c_b, wo_c, w_out, ln1_g, ln1_b,
           router_w, router_b, we_gate, we_up, we_down, ws_gate, ws_up, ws_down, w_ple_gate, w_ple, ln2_g, ln2_b):
    Bp, Sp, D = x_prompt.shape
    Bs, Ts, _ = x_sample.shape
    past_len = page_table.shape[1] * PAGE_SIZE

    pos_p = jnp.tile(jnp.arange(Sp, dtype=jnp.int32), Bp)
    pos_s = jnp.tile(past_len + jnp.arange(Ts, dtype=jnp.int32), Bs)
    tabs = []
    for pos in (pos_p, pos_s):
        cb, sb = _rope_tables(pos, ROPE_B // 2)
        cc, sc = _rope_tables(pos, DK_C // 2)
        tabs.append(dict(cos_b=cb, sin_b=sb, cos_c=cc, sin_c=sc))
    expand = (jnp.arange(N_EXPERTS)[:, None] == (jnp.arange(N_EXPERTS * D_FF_E)[None, :] // D_FF_E)).astype(BF16)
    shared = dict(router_wt=router_w.T.astype(F32), router_b=router_b.reshape(N_EXPERTS, 1).astype(F32),
                  expand=expand)

    cache_t = jnp.swapaxes(cache_mla, 2, 3)

    hp = _layer_norm(x_prompt.reshape(Bp * Sp, D), ln_emb_g, ln_emb_b)
    hs = _layer_norm(x_sample.reshape(Bs * Ts, D), ln_emb_g, ln_emb_b)
    zero_shift = jnp.zeros((Bp, A_COLS), F32)
    zero_wkv = jnp.zeros((Bp, N_A, A_WIDTH), F32)
    zero_ret = jnp.zeros((Bp, H_C, DK_C, DV_C), F32)
    wkv_tc = min(128, Sp)
    samp_bb = 8 if Bs % 8 == 0 else Bs

    outs = [[] for _ in range(8)]
    for i in range(DEPTH):
        lw = _layer_weights(i, w_in, mu_a, w0, w2, a0, a2, g2, k_k, k_a, r_k, lnx_g, lnx_b, wo_a, q_norm_g, w_qb,
                            kv_norm_g, w_uk, w_uv, wo_b, gn_c_g, gn_c_b, wo_c, w_out, ln1_g, ln1_b, we_gate, we_up,
                            we_down, ws_gate, ws_up, ws_down, w_ple_gate, w_ple, ln2_g, ln2_b)
        attend_p = lambda qcat, kv_row, kv_bf: _flash_prompt(qcat, kv_bf, Bp, Sp)
        hp, kv_p, wkv_p, sh_p, ret_p = _trunk_layer(
            hp, p_prompt[i].reshape(Bp * Sp, PLE_DIM), tabs[0], zero_shift, zero_wkv, zero_ret, attend_p, lw, shared,
            Bp, Sp, (Bp, wkv_tc), 1)
        attend_s = lambda qcat, kv_row, kv_bf, i=i: _sample_attn(qcat.astype(F32), kv_row, cache_t, page_table, i,
                                                                 Bs, Ts)
        hs, kv_s, wkv_s, sh_s, ret_s = _trunk_layer(
            hs, p_sample[i].reshape(Bs * Ts, PLE_DIM), tabs[1], state_shift[i], _wkv_to_lanes(state_wkv[i]),
            state_ret[i], attend_s, lw, shared, Bs, Ts, (samp_bb, Ts), samp_bb)
        for lst, val in zip(outs, (kv_p, kv_s, _wkv_from_lanes(wkv_p), _wkv_from_lanes(wkv_s), sh_p, sh_s,
                                   ret_p, ret_s)):
            lst.append(val)
    return (hp.reshape(Bp, Sp, D), hs.reshape(Bs, Ts, D)) + tuple(jnp.stack(l) for l in outs)
```

```python
import functools
import math

import jax
import jax.numpy as jnp
from jax import lax
from jax.experimental import pallas as pl
from jax.experimental.pallas import tpu as pltpu

F32 = jnp.float32
BF16 = jnp.bfloat16

D_MODEL = 1024
H_A, N_A = 8, 64
A_WIDTH = H_A * N_A
W_LORA, A_LORA, G_LORA = 64, 64, 128
A_COLS = 3 * A_WIDTH + W_LORA + A_LORA + G_LORA
RWKV_GN_EPS = 64e-5
WKV_CHUNK = 16
H_B, Q_LORA, KV_LORA, NOPE_B, ROPE_B, DV_B = 8, 256, 128, 64, 32, 64
KV_DIM = KV_LORA + ROPE_B
B_COLS = Q_LORA + KV_LORA + ROPE_B
B_PAD = 640
MLA_SCALE = (NOPE_B + ROPE_B) ** -0.5
H_C, DK_C, DV_C = 4, 128, 256
C_COLS = 2 * H_C * DK_C + 2 * H_C * DV_C
RET_CHUNK = 128
RET_GN_EPS = 1e-5
RET_LOG_GAMMA = tuple(math.log1p(-(2.0 ** (-5.0 - h))) for h in range(H_C))
N_BRANCH = 3
GATE_COLS = N_BRANCH * D_MODEL
ROPE_THETA = 10000.0
N_EXPERTS, N_GROUPS, E_PER_GROUP, D_FF_E, D_FF_SHARED = 16, 4, 4, 256, 256
PLE_DIM = 256
PAGE_SIZE = 128
DEPTH = 2
DN_ALPHA = (2 * DEPTH) ** 0.25
LN_EPS = 1e-5
RMS_EPS = 1e-6

VMEM_LIMIT = 56 * 1024 * 1024


def _pow2_tile(n, pref):
    t = 1
    while t * 2 <= pref and n % (t * 2) == 0:
        t *= 2
    return t


def _params(sem):
    return pltpu.CompilerParams(dimension_semantics=sem, vmem_limit_bytes=VMEM_LIMIT)


def _const_spec(shape):
    nd = len(shape)
    return pl.BlockSpec(shape, lambda *_: (0,) * nd, pipeline_mode=pl.Buffered(1))


def _sigmoid(x):
    return 1.0 / (1.0 + jnp.exp(-x))


def _ln_rows(x, g, b, eps):
    mu = jnp.mean(x, -1, keepdims=True)
    xc = x - mu
    var = jnp.mean(xc * xc, -1, keepdims=True)
    return xc * lax.rsqrt(var + eps) * g + b


def _dot(a, b):
    return jnp.dot(a, b, preferred_element_type=F32)


def _split_bf16(x):
    hi = x.astype(BF16)
    return hi, (x - hi.astype(F32)).astype(BF16)


def _dot_nt(a, b):
    return lax.dot_general(a, b, (((1,), (1,)), ((), ())), preferred_element_type=F32)


def _dot_tn(a, b):
    return lax.dot_general(a, b, (((0,), (0,)), ((), ())), preferred_element_type=F32)


def _ln_kernel(x_ref, g_ref, b_ref, o_ref):
    o_ref[...] = _ln_rows(x_ref[...], g_ref[...], b_ref[...], LN_EPS)


def _layer_norm(x, g, b):
    n, d = x.shape
    tm = _pow2_tile(n, 512)
    return pl.pallas_call(
        _ln_kernel,
        grid=(n // tm,),
        in_specs=[pl.BlockSpec((tm, d), lambda i: (i, 0)), _const_spec((1, d)), _const_spec((1, d))],
        out_specs=pl.BlockSpec((tm, d), lambda i: (i, 0)),
        out_shape=jax.ShapeDtypeStruct((n, d), F32),
        compiler_params=_params(("parallel",)),
        name="embed_ln",
    )(x, g.reshape(1, d), b.reshape(1, d))


def _proj_kernel(h_ref, wa_ref, wb_ref, wc_ref, pa_ref, pb_ref, pc_ref):
    x = h_ref[...].astype(BF16)
    pa_ref[...] = _dot(x, wa_ref[...])
    pb_ref[...] = _dot(x, wb_ref[...])
    pc_ref[...] = _dot(x, wc_ref[...])


def _proj(h, wa, wb, wc):
    n, d = h.shape
    tm = _pow2_tile(n, 256)
    row = lambda w: pl.BlockSpec((tm, w), lambda i: (i, 0))
    return pl.pallas_call(
        _proj_kernel,
        grid=(n // tm,),
        in_specs=[row(d), _const_spec(wa.shape), _const_spec(wb.shape), _const_spec(wc.shape)],
        out_specs=[row(A_COLS), row(B_PAD), row(C_COLS)],
        out_shape=[jax.ShapeDtypeStruct((n, A_COLS), F32), jax.ShapeDtypeStruct((n, B_PAD), F32),
                   jax.ShapeDtypeStruct((n, C_COLS), F32)],
        compiler_params=_params(("parallel",)),
        name="in_proj",
    )(h, wa, wb, wc)


def _head_ones(n, dtype):
    r = lax.broadcasted_iota(jnp.int32, (n, n), 0) // N_A
    c = lax.broadcasted_iota(jnp.int32, (n, n), 1) // N_A
    return (r == c).astype(dtype)


def _wkv_kernel(pa_ref, sh0_ref, s0_ref, mu_ref, w0_ref, a0_ref, wwa_ref, g2_ref, kk_ref, ka_ref, rk_ref,
                lng_ref, lnb_ref, ya_ref, sout_ref, prev_scr, prevm_scr, g_scr, *, Bb, Tc, C):
    c = pl.program_id(1)
    rows = Bb * Tc
    UW = A_WIDTH // 2
    HU = UW // N_A
    NG = 4
    R = HU * C
    RT = NG * R
    units = [(bb, u) for bb in range(Bb) for u in range(2)]
    groups = [units[i:i + NG] for i in range(0, len(units), NG)]
    state_mask = (lax.broadcasted_iota(jnp.int32, (UW, NG * UW), 0) // N_A
                  == (lax.broadcasted_iota(jnp.int32, (UW, NG * UW), 1) % UW) // N_A)

    @pl.when(c == 0)
    def _():
        prev_scr[...] = sh0_ref[...]
        for gi, grp in enumerate(groups):
            slabs = [jnp.concatenate([s0_ref[bb, :, u * UW:(u + 1) * UW]] * HU, axis=0) for bb, u in grp]
            g_scr[gi] = jnp.where(state_mask, jnp.concatenate(slabs, axis=1), 0.0)

    pa = pa_ref[...].reshape(rows, A_COLS)
    prevm_scr[...] = pltpu.roll(pa, 1, 0)
    for bb in range(Bb):
        prevm_scr[bb * Tc:bb * Tc + 1, :] = prev_scr[bb:bb + 1, :]
    for bb in range(Bb):
        prev_scr[bb:bb + 1, :] = pa[(bb + 1) * Tc - 1:(bb + 1) * Tc, :]
    xm = pa + (prevm_scr[...] - pa) * mu_ref[...]

    o1, o2, o3 = A_WIDTH, 2 * A_WIDTH, 3 * A_WIDTH
    r = xm[:, :o1]
    k = xm[:, o1:o2]
    v = xm[:, o2:o3]
    wa = xm[:, o3:o3 + W_LORA + A_LORA]
    gl = xm[:, o3 + W_LORA + A_LORA:]
    lane = lax.broadcasted_iota(jnp.int32, wa.shape, 1)
    wa = jnp.where(lane < W_LORA, jnp.tanh(wa), wa)
    lo = _dot(wa.astype(BF16), wwa_ref[...])
    nz = -(w0_ref[...] + lo[:, :A_WIDTH])
    softplus = jnp.maximum(nz, 0.0) + jnp.log(1.0 + jnp.exp(-jnp.abs(nz)))
    log_w = -jnp.exp(-softplus - 0.5)
    a = _sigmoid(a0_ref[...] + lo[:, A_WIDTH:])
    g = _dot(_sigmoid(gl).astype(BF16), g2_ref[...])

    ones_unit = _head_ones(UW, BF16)

    def segsum(x):
        hi, lo = _split_bf16(x)
        return jnp.concatenate([_dot(hi[:, s * UW:(s + 1) * UW], ones_unit) + _dot(lo[:, s * UW:(s + 1) * UW], ones_unit)
                                for s in range(A_WIDTH // UW)], axis=1)
    kk = k * kk_ref[...]
    kk = kk / jnp.maximum(jnp.sqrt(segsum(kk * kk)), 1e-12)
    k2 = k * (1.0 + (a - 1.0) * ka_ref[...])
    kka = kk * a
    bonus = segsum(r * k2 * rk_ref[...]) * v

    ti = lax.broadcasted_iota(jnp.int32, (rows, rows), 0)
    tj = lax.broadcasted_iota(jnp.int32, (rows, rows), 1)
    same_chunk = (ti // C) == (tj // C)
    lw_hi, lw_lo = _split_bf16(log_w)
    hdot = lambda m: _dot(m.astype(BF16), lw_hi) + _dot(m.astype(BF16), lw_lo)
    cs = hdot(same_chunk & (tj <= ti))
    ctot = hdot(same_chunk)
    e_inv = jnp.exp(-cs)
    e_end = jnp.exp(ctot - cs)
    e_tot = jnp.exp(ctot)
    ah = -(kk * jnp.exp(cs - log_w))
    rh = r * jnp.exp(cs)
    bh = kka * e_inv
    kh = k2 * e_inv
    b_end = kka * e_end
    k_end = k2 * e_end

    tile_mask = ((lax.broadcasted_iota(jnp.int32, (RT, UW), 0) % R) // C
                 == lax.broadcasted_iota(jnp.int32, (RT, UW), 1) // N_A)
    ri = lax.broadcasted_iota(jnp.int32, (RT, RT), 0)
    ci = lax.broadcasted_iota(jnp.int32, (RT, RT), 1)
    same_head = (ri // C) == (ci // C)
    strict = same_head & ((ci % C) < (ri % C))
    incl = same_head & ((ci % C) <= (ri % C))
    lane_unit = (lax.broadcasted_iota(jnp.int32, (NG * 2 * C, NG * UW), 0) // (2 * C)
                 == lax.broadcasted_iota(jnp.int32, (NG * 2 * C, NG * UW), 1) // UW)
    bf = lambda x: x.astype(BF16)
    n_sq = C.bit_length() - 2
    n_chunks = Tc // C

    def tile(xs):
        return jnp.where(tile_mask, jnp.concatenate([x for x in xs for _ in range(HU)], axis=0), 0.0)

    def fold(x, i):
        return functools.reduce(lambda p, q: p + q, [x[i * R + j * C:i * R + (j + 1) * C] for j in range(HU)])

    def spread(x):
        return jnp.where(lane_unit, jnp.concatenate([x] * NG, axis=1), 0.0)

    def where_rows(bb, ch, u):
        r0 = bb * Tc + ch * C
        return slice(r0, r0 + C), slice(u * UW, (u + 1) * UW)

    free = {}
    for ch in range(n_chunks):
        for gi, grp in enumerate(groups):
            at = [where_rows(bb, ch, u) for bb, u in grp]
            pick = lambda arr: [arr[sl, ln] for sl, ln in at]
            ahm, rhm, vm = tile(pick(ah)), tile(pick(rh)), tile(pick(v))
            lr = bf(jnp.concatenate([ahm, rhm], axis=0))
            gb = _dot_nt(lr, bf(tile(pick(bh))))
            gk = _dot_nt(lr, bf(tile(pick(kh))))
            x = jnp.concatenate([ahm, _dot(bf(jnp.where(strict, gk[:RT], 0.0)), bf(vm))], axis=1)
            pw = bf(jnp.where(strict, gb[:RT], 0.0))
            for i in range(n_sq + 1):
                x = x + _dot(pw, bf(x))
                if i < n_sq:
                    pw = bf(_dot(pw, pw))
            rx = _dot(bf(jnp.where(incl, gb[RT:], 0.0)), bf(x))
            yv = rx[:, UW:] + _dot(bf(jnp.where(incl, gk[RT:], 0.0)), bf(vm))
            free[ch, gi] = [(fold(x[:, :UW], i), fold(x[:, UW:], i), rh[sl, ln] + fold(rx[:, :UW], i), fold(yv, i))
                            for i, (sl, ln) in enumerate(at)]

    y_parts = {}
    for gi, grp in enumerate(groups):
        g0 = g_scr[gi]
        for ch in range(n_chunks):
            at = [where_rows(bb, ch, u) for bb, u in grp]
            fr = free[ch, gi]
            lhs = jnp.concatenate([jnp.concatenate([w_rows, r_eff], axis=0) for w_rows, _, r_eff, _ in fr], axis=0)
            uy = _dot_nt(bf(spread(lhs)), bf(g0))
            uv, bk, dec = [], [], []
            for i, ((sl, ln), (_, u_free, _, y_free)) in enumerate(zip(at, fr)):
                base = i * 2 * C
                uv += [uy[base:base + C] + u_free, v[sl, ln]]
                bk += [b_end[sl, ln], k_end[sl, ln]]
                dec.append(e_tot[sl.start:sl.start + 1, ln])
                y_parts[ch, grp[i]] = uy[base + C:base + 2 * C] + y_free
            upd = _dot_tn(bf(jnp.concatenate(uv, axis=0)), bf(spread(jnp.concatenate(bk, axis=0))))
            g0 = g0 * jnp.concatenate(dec, axis=1) + jnp.where(state_mask, upd, 0.0)
        g_scr[gi] = g0

        @pl.when(c == pl.num_programs(1) - 1)
        def _(g0=g0, grp=grp):
            for i, (bb, u) in enumerate(grp):
                g1 = g0[:, i * UW:(i + 1) * UW]
                sout_ref[bb, :, u * UW:(u + 1) * UW] = functools.reduce(
                    lambda p, q: p + q, [g1[j * N_A:(j + 1) * N_A] for j in range(HU)])

    y = jnp.concatenate([jnp.concatenate([y_parts[ch, (bb, 0)], y_parts[ch, (bb, 1)]], axis=1)
                         for bb in range(Bb) for ch in range(n_chunks)], axis=0)
    mu = segsum(y) * (1.0 / N_A)
    yc = y - mu
    var = segsum(yc * yc) * (1.0 / N_A)
    yn = yc * lax.rsqrt(var + RWKV_GN_EPS) * lng_ref[...] + lnb_ref[...]
    ya_ref[...] = ((yn + bonus) * g).reshape(Bb, Tc, A_WIDTH).astype(ya_ref.dtype)


def _wkv(pa, shift0, s0, lw, Bb, Tc, out_dtype):
    B, T, _ = pa.shape
    rows = Bb * Tc
    C = min(WKV_CHUNK, Tc)
    vec = lambda: _const_spec((1, A_WIDTH))
    return pl.pallas_call(
        functools.partial(_wkv_kernel, Bb=Bb, Tc=Tc, C=C),
        grid=(B // Bb, T // Tc),
        in_specs=[pl.BlockSpec((Bb, Tc, A_COLS), lambda b, c: (b, c, 0)),
                  pl.BlockSpec((Bb, A_COLS), lambda b, c: (b, 0)),
                  pl.BlockSpec((Bb, N_A, A_WIDTH), lambda b, c: (b, 0, 0)),
                  _const_spec((1, A_COLS)), vec(), vec(),
                  _const_spec((W_LORA + A_LORA, 2 * A_WIDTH)), _const_spec((G_LORA, A_WIDTH)),
                  vec(), vec(), vec(), vec(), vec()],
        out_specs=[pl.BlockSpec((Bb, Tc, A_WIDTH), lambda b, c: (b, c, 0)),
                   pl.BlockSpec((Bb, N_A, A_WIDTH), lambda b, c: (b, 0, 0))],
        out_shape=[jax.ShapeDtypeStruct((B, T, A_WIDTH), out_dtype),
                   jax.ShapeDtypeStruct((B, N_A, A_WIDTH), F32)],
        scratch_shapes=[pltpu.VMEM((Bb, A_COLS), F32), pltpu.VMEM((rows, A_COLS), F32),
                        pltpu.VMEM((Bb // 2, A_WIDTH // 2, 2 * A_WIDTH), F32)],
        compiler_params=_params(("parallel", "arbitrary")),
        name="rwkv_mix",
    )(pa, shift0, s0, lw["mu_a"], lw["w0"], lw["a0"], lw["w_wa"], lw["g2"], lw["k_k"], lw["k_a"], lw["r_k"],
      lw["lnx_g"], lw["lnx_b"])


def _mla_pre_kernel(pb_ref, cos_ref, sin_ref, qg_ref, kvg_ref, wqn_ref, wqpe_ref, wqps_ref, wuk_ref,
                    qcat_ref, kv_ref, kvb_ref):
    pb = pb_ref[...]
    xq = pb[:, :Q_LORA]
    cq = xq * lax.rsqrt(jnp.mean(xq * xq, -1, keepdims=True) + RMS_EPS) * qg_ref[...]
    cqb = cq.astype(BF16)
    qn = _dot(cqb, wqn_ref[...])
    qlat = _dot(qn.astype(BF16), wuk_ref[...])
    cos = cos_ref[...]
    sin = sin_ref[...]
    for h in range(H_B):
        qpe = _dot(cqb, wqpe_ref[h]) * cos + _dot(cqb, wqps_ref[h]) * sin
        qcat_ref[h, :, :KV_LORA] = qlat[:, h * KV_LORA:(h + 1) * KV_LORA].astype(BF16)
        qcat_ref[h, :, KV_LORA:] = qpe.astype(BF16)
    xkv = pb[:, Q_LORA:Q_LORA + KV_LORA]
    ckv = xkv * lax.rsqrt(jnp.mean(xkv * xkv, -1, keepdims=True) + RMS_EPS) * kvg_ref[...]
    kpe = pb[:, Q_LORA + KV_LORA:B_COLS] * cos + pb[:, 512:512 + ROPE_B] * sin
    kv_ref[:, :KV_LORA] = ckv
    kv_ref[:, KV_LORA:] = kpe
    kvb_ref[:, :KV_LORA] = ckv.astype(BF16)
    kvb_ref[:, KV_LORA:] = kpe.astype(BF16)


def _mla_pre(pb, cos, sin, lw):
    n = pb.shape[0]
    tm = _pow2_tile(n, 256)
    row = lambda w: pl.BlockSpec((tm, w), lambda i: (i, 0))
    return pl.pallas_call(
        _mla_pre_kernel,
        grid=(n // tm,),
        in_specs=[row(B_PAD), row(ROPE_B), row(ROPE_B), _const_spec((1, Q_LORA)), _const_spec((1, KV_LORA)),
                  _const_spec((Q_LORA, H_B * NOPE_B)), _const_spec((H_B, Q_LORA, ROPE_B)),
                  _const_spec((H_B, Q_LORA, ROPE_B)), _const_spec((H_B * NOPE_B, H_B * KV_LORA))],
        out_specs=[pl.BlockSpec((H_B, tm, KV_DIM), lambda i: (0, i, 0)), row(KV_DIM), row(KV_DIM)],
        out_shape=[jax.ShapeDtypeStruct((H_B, n, KV_DIM), BF16), jax.ShapeDtypeStruct((n, KV_DIM), F32),
                   jax.ShapeDtypeStruct((n, KV_DIM), BF16)],
        compiler_params=_params(("parallel",)),
        name="mla_pre",
    )(pb, cos, sin, lw["q_norm_g"], lw["kv_norm_g"], lw["w_qn"], lw["w_qpe"], lw["w_qpe_sw"], lw["w_uk_bd"])


def _softmax_step(carry, s, vals):
    m, l, acc = carry
    m_new = jnp.maximum(m, jnp.max(s, axis=-1, keepdims=True))
    p = jnp.exp(s - m_new)
    alpha = jnp.exp(m - m_new)
    l = alpha * l + jnp.sum(p, axis=-1, keepdims=True)
    acc = alpha * acc + _dot(p.astype(vals.dtype), vals)
    return m_new, l, acc


def _flash_kernel(q_ref, kv_ref, o_ref, *, BQ, BK):
    i = pl.program_id(1)
    nfull = (i * BQ) // BK
    c2 = MLA_SCALE * math.log2(math.e)

    rows = H_B * BQ
    q = q_ref[...].reshape(rows, KV_DIM)

    def block(j, carry, masked):
        m, l, acc = carry
        kvj = kv_ref[pl.ds(pl.multiple_of(j * BK, BK), BK), :]
        s = _dot_nt(q, kvj)
        if masked:
            qpos = i * BQ + (lax.broadcasted_iota(jnp.int32, (rows, BK), 0) % BQ)
            kpos = j * BK + lax.broadcasted_iota(jnp.int32, (rows, BK), 1)
            s = jnp.where(kpos <= qpos, s, -jnp.inf)
        m_new = jnp.maximum(m, jnp.max(s, axis=-1, keepdims=True) * c2)
        p = jnp.exp2(s * c2 - m_new)
        alpha = jnp.exp2(m - m_new)
        l = alpha * l + jnp.sum(p, axis=-1, keepdims=True)
        acc = alpha * acc + _dot(p.astype(BF16), kvj[:, :KV_LORA])
        return m_new, l, acc

    init = (jnp.full((rows, 1), -jnp.inf, F32), jnp.zeros((rows, 1), F32), jnp.zeros((rows, KV_LORA), F32))
    carry = lax.fori_loop(0, nfull, lambda j, c: block(j, c, False), init)
    m, l, acc = block(nfull, carry, True)
    o = acc / l
    for h in range(H_B):
        o_ref[:, h * KV_LORA:(h + 1) * KV_LORA] = o[h * BQ:(h + 1) * BQ]


def _flash_prompt(qcat, kvb, B, T):
    BQ = min(128, T)
    BK = min(512, T)
    nq = T // BQ
    return pl.pallas_call(
        functools.partial(_flash_kernel, BQ=BQ, BK=BK),
        grid=(B, nq),
        in_specs=[pl.BlockSpec((H_B, BQ, KV_DIM), lambda b, i: (0, b * nq + i, 0)),
                  pl.BlockSpec((T, KV_DIM), lambda b, i: (b, 0))],
        out_specs=pl.BlockSpec((BQ, H_B * KV_LORA), lambda b, i: (b * nq + i, 0)),
        out_shape=jax.ShapeDtypeStruct((B * T, H_B * KV_LORA), F32),
        compiler_params=_params(("parallel", "arbitrary")),
        name="mla_prompt_attn",
    )(qcat, kvb)


def _sample_attn_kernel(ptc_ref, ptn_ref, q_ref, kvn_ref, cache_ref, o_ref, buf, kvb_scr, s_scr, sem,
                        *, layer, B, Ts, PG, npages):
    b = pl.program_id(0)
    slot = lax.rem(b, 2)
    rows = H_B * Ts
    width = PG * PAGE_SIZE
    nchunks = npages // PG
    lanes = 128

    def page_copy(page, sl, p):
        return pltpu.make_async_copy(cache_ref.at[layer, page], buf.at[sl, p], sem.at[sl])

    def start(pt_ref, sl):
        for p in range(npages):
            page_copy(pt_ref[0, 0, p], sl, p).start()

    @pl.when(b == 0)
    def _():
        start(ptc_ref, 0)

    @pl.when(b + 1 < B)
    def _():
        start(ptn_ref, 1 - slot)

    for p in range(npages):
        page_copy(0, slot, p).wait()

    q32 = q_ref[...].reshape(rows, KV_DIM)
    qb = q32.astype(BF16)
    mx = jnp.full((rows, lanes), -jnp.inf, F32)
    for c in range(nchunks):
        kvt = jnp.concatenate([buf[slot, c * PG + p].astype(BF16) for p in range(PG)], axis=1)
        kvb_scr[:, c * width:(c + 1) * width] = kvt
        s = _dot(qb, kvt) * MLA_SCALE
        s_scr[:, c * width:(c + 1) * width] = s
        for j in range(width // lanes):
            mx = jnp.maximum(mx, s[:, j * lanes:(j + 1) * lanes])
    kvn = kvn_ref[...]
    s_new = _dot_nt(q32, kvn) * MLA_SCALE
    tq = lax.broadcasted_iota(jnp.int32, (rows, Ts), 0) % Ts
    tk = lax.broadcasted_iota(jnp.int32, (rows, Ts), 1)
    s_new = jnp.where(tk <= tq, s_new, -jnp.inf)
    m = jnp.maximum(jnp.max(mx, axis=-1, keepdims=True), jnp.max(s_new, axis=-1, keepdims=True))
    lsum = jnp.zeros((rows, lanes), F32)
    acc = jnp.zeros((rows, KV_LORA), F32)
    for c in range(nchunks):
        p = jnp.exp(s_scr[:, c * width:(c + 1) * width] - m)
        for j in range(width // lanes):
            lsum = lsum + p[:, j * lanes:(j + 1) * lanes]
        acc = acc + _dot_nt(p.astype(BF16), kvb_scr[:KV_LORA, c * width:(c + 1) * width])
    p_new = jnp.exp(s_new - m)
    l = jnp.sum(lsum, axis=-1, keepdims=True) + jnp.sum(p_new, axis=-1, keepdims=True)
    o = (acc + _dot(p_new, kvn[:, :KV_LORA])) / l
    for h in range(H_B):
        o_ref[:, h * KV_LORA:(h + 1) * KV_LORA] = o[h * Ts:(h + 1) * Ts]


def _sample_attn(qcat32, kv_new, cache_t, page_table, layer, B, Ts):
    npages = page_table.shape[1]
    PG = _pow2_tile(npages, 16)
    past = npages * PAGE_SIZE
    pt = page_table.reshape(B, 1, npages)
    pt_spec = lambda f: pl.BlockSpec((1, 1, npages), f, memory_space=pltpu.SMEM)
    return pl.pallas_call(
        functools.partial(_sample_attn_kernel, layer=layer, B=B, Ts=Ts, PG=PG, npages=npages),
        grid=(B,),
        in_specs=[pt_spec(lambda b: (b, 0, 0)), pt_spec(lambda b: (jnp.minimum(b + 1, B - 1), 0, 0)),
                  pl.BlockSpec((H_B, Ts, KV_DIM), lambda b: (0, b, 0)),
                  pl.BlockSpec((Ts, KV_DIM), lambda b: (b, 0)),
                  pl.BlockSpec(memory_space=pl.ANY)],
        out_specs=pl.BlockSpec((Ts, H_B * KV_LORA), lambda b: (b, 0)),
        out_shape=jax.ShapeDtypeStruct((B * Ts, H_B * KV_LORA), F32),
        scratch_shapes=[pltpu.VMEM((2, npages, KV_DIM, PAGE_SIZE), F32), pltpu.VMEM((KV_DIM, past), BF16),
                        pltpu.VMEM((H_B * Ts, past), F32), pltpu.SemaphoreType.DMA((2,))],
        compiler_params=_params(("arbitrary",)),
        name="mla_sample_attn",
    )(pt, pt, qcat32, kv_new, cache_t)


def _ret_kernel(pc_ref, cos_ref, sin_ref, r0_ref, gng_ref, gnb_ref, yc_ref, rout_ref, *, Bb, C):
    c = pl.program_id(1)

    @pl.when(c == 0)
    def _():
        rout_ref[...] = r0_ref[...]

    mm = BF16 if C >= 16 else F32
    pc = pc_ref[...]
    cos = cos_ref[...]
    sin = sin_ref[...]
    qk = H_C * DK_C
    vw = H_C * DV_C
    diff = (lax.broadcasted_iota(jnp.int32, (C, C), 0) - lax.broadcasted_iota(jnp.int32, (C, C), 1)).astype(F32)
    row_v = lax.broadcasted_iota(jnp.int32, (C, DV_C), 0).astype(F32)
    row_k = lax.broadcasted_iota(jnp.int32, (C, DK_C), 0).astype(F32)
    for h in range(H_C):
        lg = RET_LOG_GAMMA[h]
        dmat = jnp.where(diff >= 0, jnp.exp(lg * jnp.maximum(diff, 0.0)), 0.0)
        row_decay = jnp.exp((row_v + 1.0) * lg)
        k_decay = jnp.exp((C - 1.0 - row_k) * lg)
        chunk_decay = math.exp(C * lg)
        qh = pc[:, h * DK_C:(h + 1) * DK_C]
        kh = pc[:, qk + h * DK_C:qk + (h + 1) * DK_C]
        q = qh * cos + pltpu.roll(qh, DK_C // 2, 1) * sin
        k = (kh * cos + pltpu.roll(kh, DK_C // 2, 1) * sin) * DK_C ** -0.5
        v = pc[:, 2 * qk + h * DV_C:2 * qk + (h + 1) * DV_C]
        g = pc[:, 2 * qk + vw + h * DV_C:2 * qk + vw + (h + 1) * DV_C]
        for bb in range(Bb):
            sl = slice(bb * C, (bb + 1) * C)
            qb = q[sl].astype(mm)
            kb = k[sl]
            vb = v[sl].astype(mm)
            att = _dot_nt(qb, kb.astype(mm)) * dmat
            state = rout_ref[bb, h]
            o = _dot(att.astype(mm), vb) + _dot(qb, state.astype(mm)) * row_decay
            rout_ref[bb, h] = state * chunk_decay + _dot_tn((kb * k_decay).astype(mm), vb)
            mu = jnp.mean(o, -1, keepdims=True)
            oc = o - mu
            var = jnp.mean(oc * oc, -1, keepdims=True)
            on = (oc * lax.rsqrt(var + RET_GN_EPS) * gng_ref[:, h * DV_C:(h + 1) * DV_C]
                  + gnb_ref[:, h * DV_C:(h + 1) * DV_C])
            gg = g[sl]
            yc_ref[sl, h * DV_C:(h + 1) * DV_C] = on * (gg * _sigmoid(gg))


def _retention(pc, cos, sin, r0, lw, B, T, Bb):
    C = min(RET_CHUNK, T)
    nc = T // C
    rows = Bb * C
    row = lambda w: pl.BlockSpec((rows, w), lambda b, c: (b * nc + c, 0))
    st = pl.BlockSpec((Bb, H_C, DK_C, DV_C), lambda b, c: (b, 0, 0, 0))
    return pl.pallas_call(
        functools.partial(_ret_kernel, Bb=Bb, C=C),
        grid=(B // Bb, nc),
        in_specs=[row(C_COLS), row(DK_C), row(DK_C), st, _const_spec((1, H_C * DV_C)), _const_spec((1, H_C * DV_C))],
        out_specs=[row(H_C * DV_C), st],
        out_shape=[jax.ShapeDtypeStruct((B * T, H_C * DV_C), F32), jax.ShapeDtypeStruct(r0.shape, F32)],
        compiler_params=_params(("parallel", "arbitrary")),
        name="retention",
    )(pc, cos, sin, r0, lw["gn_c_g"], lw["gn_c_b"])


def _merge_kernel(h_ref, ya_ref, ol_ref, yc_ref, wg_ref, woa_ref, wuv_ref, wob_ref, woc_ref, wout_ref,
                  g_ref, b_ref, o_ref):
    h = h_ref[...]
    gate = _sigmoid(_dot(h.astype(BF16), wg_ref[...]))
    ya = _dot(ya_ref[...].astype(BF16), woa_ref[...])
    ob = _dot(ol_ref[...].astype(BF16), wuv_ref[...])
    yb = _dot(ob.astype(BF16), wob_ref[...])
    yc = _dot(yc_ref[...].astype(BF16), woc_ref[...])
    d = D_MODEL
    merged = gate[:, :d] * ya + gate[:, d:2 * d] * yb + gate[:, 2 * d:] * yc
    mix = _dot(merged.astype(BF16), wout_ref[...])
    o_ref[...] = _ln_rows(DN_ALPHA * h + mix, g_ref[...], b_ref[...], LN_EPS)


def _merge(h, ya, olat, yc, lw):
    n, d = h.shape
    tm = _pow2_tile(n, 256)
    row = lambda w: pl.BlockSpec((tm, w), lambda i: (i, 0))
    ws = [lw["w_gate"], lw["wo_a"], lw["w_uv_bd"], lw["wo_b"], lw["wo_c"], lw["w_out"], lw["ln1_g"], lw["ln1_b"]]
    return pl.pallas_call(
        _merge_kernel,
        grid=(n // tm,),
        in_specs=[row(d), row(A_WIDTH), row(H_B * KV_LORA), row(H_C * DV_C)] + [_const_spec(w.shape) for w in ws],
        out_specs=row(d),
        out_shape=jax.ShapeDtypeStruct((n, d), F32),
        compiler_params=_params(("parallel",)),
        name="merge_ln1",
    )(h, ya, olat, yc, *ws)


def _top2_sum(a, b, c, d):
    m_ab, n_ab = jnp.maximum(a, b), jnp.minimum(a, b)
    m_cd, n_cd = jnp.maximum(c, d), jnp.minimum(c, d)
    return jnp.maximum(m_ab, m_cd) + jnp.maximum(jnp.minimum(m_ab, m_cd), jnp.maximum(n_ab, n_cd))


def _ffn_kernel(h_ref, p_ref, rwt_ref, rb_ref, wpg_ref, wp_ref, weg_ref, weu_ref, wed_ref, wsg_ref, wsu_ref,
                wsd_ref, exp_ref, g_ref, b_ref, o_ref):
    h = h_ref[...]
    hb = h.astype(BF16)
    tm = h.shape[0]
    ple = _sigmoid(_dot(hb, wpg_ref[...])) * _dot(p_ref[...].astype(BF16), wp_ref[...])

    logits = lax.dot_general(rwt_ref[...], h, (((1,), (1,)), ((), ())), precision=lax.Precision.HIGHEST,
                             preferred_element_type=F32)
    score = _sigmoid(logits)
    sel = score + rb_ref[...]
    srow = [sel[e:e + 1] for e in range(N_EXPERTS)]
    grp_score = [_top2_sum(*srow[E_PER_GROUP * g:E_PER_GROUP * (g + 1)]) for g in range(N_GROUPS)]
    best = grp_score[0]
    gidx = jnp.zeros((1, tm), jnp.int32)
    for g in range(1, N_GROUPS):
        upd = grp_score[g] > best
        gidx = jnp.where(upd, g, gidx)
        best = jnp.where(upd, grp_score[g], best)
    wrow = []
    for e in range(N_EXPERTS):
        g = e // E_PER_GROUP
        rank = jnp.zeros((1, tm), jnp.int32)
        for e2 in range(E_PER_GROUP * g, E_PER_GROUP * (g + 1)):
            if e2 < e:
                rank += (srow[e2] >= srow[e]).astype(jnp.int32)
            elif e2 > e:
                rank += (srow[e2] > srow[e]).astype(jnp.int32)
        chosen = (gidx == g) & (rank < 2)
        wrow.append(jnp.where(chosen, score[e:e + 1], 0.0))
    wsum = wrow[0]
    for e in range(1, N_EXPERTS):
        wsum = wsum + wrow[e]
    gates = jnp.concatenate(wrow, axis=0) / wsum
    gt = gates.T
    g_hi = gt.astype(BF16)
    g_lo = (gt - g_hi.astype(F32)).astype(BF16)

    routed = jnp.zeros((tm, D_MODEL), F32)
    cw = E_PER_GROUP * D_FF_E
    for g in range(N_GROUPS):
        cs = slice(g * cw, (g + 1) * cw)
        hg = _dot(hb, weg_ref[:, cs])
        hu = _dot(hb, weu_ref[:, cs])
        gfull = _dot(g_hi, exp_ref[:, cs]) + _dot(g_lo, exp_ref[:, cs])
        act = (hg * _sigmoid(hg)) * hu * gfull
        routed = routed + _dot(act.astype(BF16), wed_ref[cs, :])
    sg = _dot(hb, wsg_ref[...])
    shared = _dot(((sg * _sigmoid(sg)) * _dot(hb, wsu_ref[...])).astype(BF16), wsd_ref[...])
    o_ref[...] = _ln_rows(DN_ALPHA * h + (routed + shared) + ple, g_ref[...], b_ref[...], LN_EPS)


def _ffn(h, p, lw, shared):
    n, d = h.shape
    tm = _pow2_tile(n, 256)
    row = lambda w: pl.BlockSpec((tm, w), lambda i: (i, 0))
    ws = [shared["router_wt"], shared["router_b"], lw["w_ple_gate"], lw["w_ple"], lw["we_gate"], lw["we_up"],
          lw["we_down"], lw["ws_gate"], lw["ws_up"], lw["ws_down"], shared["expand"], lw["ln2_g"], lw["ln2_b"]]
    return pl.pallas_call(
        _ffn_kernel,
        grid=(n // tm,),
        in_specs=[row(d), row(PLE_DIM)] + [_const_spec(w.shape) for w in ws],
        out_specs=row(d),
        out_shape=jax.ShapeDtypeStruct((n, d), F32),
        compiler_params=_params(("parallel",)),
        name="ple_moe_ln2",
    )(h, p, *ws)


def _block_diag(blocks):
    nb = len(blocks)
    r, c = blocks[0].shape
    out = jnp.zeros((nb * r, nb * c), blocks[0].dtype)
    for i, b in enumerate(blocks):
        out = out.at[i * r:(i + 1) * r, i * c:(i + 1) * c].set(b)
    return out


def _layer_weights(i, w_in, mu_a, w0, w2, a0, a2, g2, k_k, k_a, r_k, lnx_g, lnx_b, wo_a, q_norm_g, w_qb, kv_norm_g,
                   w_uk, w_uv, wo_b, gn_c_g, gn_c_b, wo_c, w_out, ln1_g, ln1_b, we_gate, we_up, we_down, ws_gate,
                   ws_up, ws_down, w_ple_gate, w_ple, ln2_g, ln2_b):
    win = w_in[i]
    o0, o1, o2 = A_COLS, A_COLS + B_COLS, A_COLS + B_COLS + C_COLS
    wb = win[:, o0:o1]
    half = ROPE_B // 2
    kpe0 = Q_LORA + KV_LORA
    wb_pad = jnp.zeros((D_MODEL, B_PAD), F32).at[:, :B_COLS].set(wb)
    wb_pad = wb_pad.at[:, 512:512 + half].set(wb[:, kpe0 + half:kpe0 + ROPE_B])
    wb_pad = wb_pad.at[:, 512 + half:512 + ROPE_B].set(wb[:, kpe0:kpe0 + half])
    zeros = jnp.zeros((W_LORA, A_WIDTH), F32)
    w_wa = jnp.concatenate([jnp.concatenate([w2[i], zeros], 1), jnp.concatenate([zeros, a2[i]], 1)], 0)
    wq = w_qb[i].reshape(Q_LORA, H_B, NOPE_B + ROPE_B)
    wq_pe = wq[:, :, NOPE_B:].transpose(1, 0, 2)
    wq_pe_sw = jnp.concatenate([wq_pe[..., half:], wq_pe[..., :half]], -1)
    vec = lambda x: x.reshape(1, -1).astype(F32)
    return dict(
        w_a=win[:, :o0].astype(BF16), w_b=wb_pad.astype(BF16), w_c=win[:, o1:o2].astype(BF16),
        w_gate=win[:, o2:].astype(BF16),
        mu_a=vec(mu_a[i]), w0=vec(w0[i]), a0=vec(a0[i]), w_wa=w_wa.astype(BF16), g2=g2[i].astype(BF16),
        k_k=vec(k_k[i]), k_a=vec(k_a[i]), r_k=vec(r_k[i]), lnx_g=vec(lnx_g[i]), lnx_b=vec(lnx_b[i]),
        wo_a=wo_a[i].astype(BF16),
        q_norm_g=vec(q_norm_g[i]), kv_norm_g=vec(kv_norm_g[i]),
        w_qn=wq[:, :, :NOPE_B].reshape(Q_LORA, H_B * NOPE_B).astype(BF16),
        w_qpe=wq_pe.astype(BF16), w_qpe_sw=wq_pe_sw.astype(BF16),
        w_uk_bd=_block_diag([w_uk[i][:, h, :].T for h in range(H_B)]).astype(BF16),
        w_uv_bd=_block_diag([w_uv[i][:, h, :] for h in range(H_B)]).astype(BF16),
        wo_b=wo_b[i].astype(BF16),
        gn_c_g=vec(gn_c_g[i]), gn_c_b=vec(gn_c_b[i]), wo_c=wo_c[i].astype(BF16),
        w_out=w_out[i].astype(BF16), ln1_g=vec(ln1_g[i]), ln1_b=vec(ln1_b[i]),
        we_gate=we_gate[i].transpose(1, 0, 2).reshape(D_MODEL, N_EXPERTS * D_FF_E).astype(BF16),
        we_up=we_up[i].transpose(1, 0, 2).reshape(D_MODEL, N_EXPERTS * D_FF_E).astype(BF16),
        we_down=we_down[i].reshape(N_EXPERTS * D_FF_E, D_MODEL).astype(BF16),
        ws_gate=ws_gate[i].astype(BF16), ws_up=ws_up[i].astype(BF16), ws_down=ws_down[i].astype(BF16),
        w_ple_gate=w_ple_gate[i].astype(BF16), w_ple=w_ple[i].astype(BF16),
        ln2_g=vec(ln2_g[i]), ln2_b=vec(ln2_b[i]),
    )


def _rope_tables(pos, half):
    inv = ROPE_THETA ** (-jnp.arange(half, dtype=F32) / half)
    ang = pos.astype(F32)[:, None] * inv[None, :]
    cos, sin = jnp.cos(ang), jnp.sin(ang)
    return jnp.concatenate([cos, cos], -1), jnp.concatenate([-sin, sin], -1)


def _wkv_to_lanes(s):
    B = s.shape[0]
    return s.transpose(0, 2, 1, 3).reshape(B, N_A, A_WIDTH)


def _wkv_from_lanes(s):
    B = s.shape[0]
    return s.reshape(B, N_A, H_A, N_A).transpose(0, 2, 1, 3)


def _trunk_layer(h, p, tabs, shift0, wkv0, ret0, attend, lw, shared, B, T, wkv_blk, ret_bb):
    pa, pb, pc = _proj(h, lw["w_a"], lw["w_b"], lw["w_c"])
    ya, wkv = _wkv(pa.reshape(B, T, A_COLS), shift0, wkv0, lw, wkv_blk[0], wkv_blk[1], F32)
    qcat, kv_row, kv_bf = _mla_pre(pb, tabs["cos_b"], tabs["sin_b"], lw)
    olat = attend(qcat, kv_row, kv_bf)
    yc, ret = _retention(pc, tabs["cos_c"], tabs["sin_c"], ret0, lw, B, T, ret_bb)
    h1 = _merge(h, ya.reshape(B * T, A_WIDTH), olat, yc, lw)
    h2 = _ffn(h1, p, lw, shared)
    shift = pa.reshape(B, T, A_COLS)[:, -1]
    return h2, kv_row.reshape(B, T, KV_DIM), wkv, shift, ret


def kernel(x_prompt, x_sample, cache_mla, state_wkv, state_shift, state_ret, page_table, p_prompt, p_sample,
           ln_emb_g, ln_emb_b, w_in, mu_a, w0, w2, a0, a2, g2, k_k, k_a, r_k, lnx_g, lnx_b, wo_a,
           q_norm_g, w_qb, kv_norm_g, w_uk, w_uv, wo_b, gn_c_g, gn_c_b, wo_c, w_out, ln1_g, ln1_b,
           router_w, router_b, we_gate, we_up, we_down, ws_gate, ws_up, ws_down, w_ple_gate, w_ple, ln2_g, ln2_b):
    Bp, Sp, D = x_prompt.shape
    Bs, Ts, _ = x_sample.shape
    past_len = page_table.shape[1] * PAGE_SIZE

    pos_p = jnp.tile(jnp.arange(Sp, dtype=jnp.int32), Bp)
    pos_s = jnp.tile(past_len + jnp.arange(Ts, dtype=jnp.int32), Bs)
    tabs = []
    for pos in (pos_p, pos_s):
        cb, sb = _rope_tables(pos, ROPE_B // 2)
        cc, sc = _rope_tables(pos, DK_C // 2)
        tabs.append(dict(cos_b=cb, sin_b=sb, cos_c=cc, sin_c=sc))
    expand = (jnp.arange(N_EXPERTS)[:, None] == (jnp.arange(N_EXPERTS * D_FF_E)[None, :] // D_FF_E)).astype(BF16)
    shared = dict(router_wt=router_w.T.astype(F32), router_b=router_b.reshape(N_EXPERTS, 1).astype(F32),
                  expand=expand)

    cache_t = jnp.swapaxes(cache_mla, 2, 3)

    hp = _layer_norm(x_prompt.reshape(Bp * Sp, D), ln_emb_g, ln_emb_b)
    hs = _layer_norm(x_sample.reshape(Bs * Ts, D), ln_emb_g, ln_emb_b)
    zero_shift = jnp.zeros((Bp, A_COLS), F32)
    zero_wkv = jnp.zeros((Bp, N_A, A_WIDTH), F32)
    zero_ret = jnp.zeros((Bp, H_C, DK_C, DV_C), F32)
    wkv_tc = min(64, Sp)
    samp_bb = 8 if Bs % 8 == 0 else Bs

    outs = [[] for _ in range(8)]
    for i in range(DEPTH):
        lw = _layer_weights(i, w_in, mu_a, w0, w2, a0, a2, g2, k_k, k_a, r_k, lnx_g, lnx_b, wo_a, q_norm_g, w_qb,
                            kv_norm_g, w_uk, w_uv, wo_b, gn_c_g, gn_c_b, wo_c, w_out, ln1_g, ln1_b, we_gate, we_up,
                            we_down, ws_gate, ws_up, ws_down, w_ple_gate, w_ple, ln2_g, ln2_b)
        attend_p = lambda qcat, kv_row, kv_bf: _flash_prompt(qcat, kv_bf, Bp, Sp)
        hp, kv_p, wkv_p, sh_p, ret_p = _trunk_layer(
            hp, p_prompt[i].reshape(Bp * Sp, PLE_DIM), tabs[0], zero_shift, zero_wkv, zero_ret, attend_p, lw, shared,
            Bp, Sp, (Bp, wkv_tc), 1)
        attend_s = lambda qcat, kv_row, kv_bf, i=i: _sample_attn(qcat.astype(F32), kv_row, cache_t, page_table, i,
                                                                 Bs, Ts)
        hs, kv_s, wkv_s, sh_s, ret_s = _trunk_layer(
            hs, p_sample[i].reshape(Bs * Ts, PLE_DIM), tabs[1], state_shift[i], _wkv_to_lanes(state_wkv[i]),
            state_ret[i], attend_s, lw, shared, Bs, Ts, (samp_bb, Ts), samp_bb)
        for lst, val in zip(outs, (kv_p, kv_s, _wkv_from_lanes(wkv_p), _wkv_from_lanes(wkv_s), sh_p, sh_s,
                                   ret_p, ret_s)):
            lst.append(val)
    return (hp.reshape(Bp, Sp, D), hs.reshape(Bs, Ts, D)) + tuple(jnp.stack(l) for l in outs)
```

```python
import functools
import math

import jax
import jax.numpy as jnp
from jax import lax
from jax.experimental import pallas as pl
from jax.experimental.pallas import tpu as pltpu

F32 = jnp.float32
BF16 = jnp.bfloat16

D_MODEL = 1024
H_A, N_A = 8, 64
A_WIDTH = H_A * N_A
W_LORA, A_LORA, G_LORA = 64, 64, 128
A_COLS = 3 * A_WIDTH + W_LORA + A_LORA + G_LORA
RWKV_GN_EPS = 64e-5
WKV_CHUNK = 16
H_B, Q_LORA, KV_LORA, NOPE_B, ROPE_B, DV_B = 8, 256, 128, 64, 32, 64
KV_DIM = KV_LORA + ROPE_B
B_COLS = Q_LORA + KV_LORA + ROPE_B
B_PAD = 640
MLA_SCALE = (NOPE_B + ROPE_B) ** -0.5
H_C, DK_C, DV_C = 4, 128, 256
C_COLS = 2 * H_C * DK_C + 2 * H_C * DV_C
RET_CHUNK = 128
RET_GN_EPS = 1e-5
RET_LOG_GAMMA = tuple(math.log1p(-(2.0 ** (-5.0 - h))) for h in range(H_C))
N_BRANCH = 3
GATE_COLS = N_BRANCH * D_MODEL
ROPE_THETA = 10000.0
N_EXPERTS, N_GROUPS, E_PER_GROUP, D_FF_E, D_FF_SHARED = 16, 4, 4, 256, 256
MOE_SLOT_NUM = 11
PLE_DIM = 256
PAGE_SIZE = 128
DEPTH = 2
DN_ALPHA = (2 * DEPTH) ** 0.25
LN_EPS = 1e-5
RMS_EPS = 1e-6

VMEM_LIMIT = 56 * 1024 * 1024


def _pow2_tile(n, pref):
    t = 1
    while t * 2 <= pref and n % (t * 2) == 0:
        t *= 2
    return t


def _params(sem):
    return pltpu.CompilerParams(dimension_semantics=sem, vmem_limit_bytes=VMEM_LIMIT)


def _const_spec(shape):
    nd = len(shape)
    return pl.BlockSpec(shape, lambda *_: (0,) * nd, pipeline_mode=pl.Buffered(1))


def _sigmoid(x):
    return 1.0 / (1.0 + jnp.exp(-x))


def _ln_rows(x, g, b, eps):
    mu = jnp.mean(x, -1, keepdims=True)
    xc = x - mu
    var = jnp.mean(xc * xc, -1, keepdims=True)
    return xc * lax.rsqrt(var + eps) * g + b


def _dot(a, b):
    return jnp.dot(a, b, preferred_element_type=F32)


def _split_bf16(x):
    hi = x.astype(BF16)
    return hi, (x - hi.astype(F32)).astype(BF16)


def _dot_nt(a, b):
    return lax.dot_general(a, b, (((1,), (1,)), ((), ())), preferred_element_type=F32)


def _dot_tn(a, b):
    return lax.dot_general(a, b, (((0,), (0,)), ((), ())), preferred_element_type=F32)


def _ln_kernel(x_ref, g_ref, b_ref, o_ref):
    o_ref[...] = _ln_rows(x_ref[...], g_ref[...], b_ref[...], LN_EPS)


def _layer_norm(x, g, b):
    n, d = x.shape
    tm = _pow2_tile(n, 512)
    return pl.pallas_call(
        _ln_kernel,
        grid=(n // tm,),
        in_specs=[pl.BlockSpec((tm, d), lambda i: (i, 0)), _const_spec((1, d)), _const_spec((1, d))],
        out_specs=pl.BlockSpec((tm, d), lambda i: (i, 0)),
        out_shape=jax.ShapeDtypeStruct((n, d), F32),
        compiler_params=_params(("parallel",)),
        name="embed_ln",
    )(x, g.reshape(1, d), b.reshape(1, d))


def _proj_kernel(h_ref, wa_ref, wb_ref, wc_ref, pa_ref, pb_ref, pc_ref):
    x = h_ref[...].astype(BF16)
    pa_ref[...] = _dot(x, wa_ref[...])
    pb_ref[...] = _dot(x, wb_ref[...])
    pc_ref[...] = _dot(x, wc_ref[...])


def _proj(h, wa, wb, wc):
    n, d = h.shape
    tm = _pow2_tile(n, 256)
    row = lambda w: pl.BlockSpec((tm, w), lambda i: (i, 0))
    return pl.pallas_call(
        _proj_kernel,
        grid=(n // tm,),
        in_specs=[row(d), _const_spec(wa.shape), _const_spec(wb.shape), _const_spec(wc.shape)],
        out_specs=[row(A_COLS), row(B_PAD), row(C_COLS)],
        out_shape=[jax.ShapeDtypeStruct((n, A_COLS), F32), jax.ShapeDtypeStruct((n, B_PAD), F32),
                   jax.ShapeDtypeStruct((n, C_COLS), F32)],
        compiler_params=_params(("parallel",)),
        name="in_proj",
    )(h, wa, wb, wc)


def _head_ones(n, dtype):
    r = lax.broadcasted_iota(jnp.int32, (n, n), 0) // N_A
    c = lax.broadcasted_iota(jnp.int32, (n, n), 1) // N_A
    return (r == c).astype(dtype)


def _wkv_kernel(pa_ref, sh0_ref, s0_ref, mu_ref, w0_ref, a0_ref, wwa_ref, g2_ref, kk_ref, ka_ref, rk_ref,
                lng_ref, lnb_ref, ya_ref, sout_ref, prev_scr, prevm_scr, g_scr, *, Bb, Tc, C):
    c = pl.program_id(1)
    rows = Bb * Tc
    UW = A_WIDTH // 2
    HU = UW // N_A
    NG = 4
    R = HU * C
    RT = NG * R
    units = [(bb, u) for bb in range(Bb) for u in range(2)]
    groups = [units[i:i + NG] for i in range(0, len(units), NG)]
    state_mask = (lax.broadcasted_iota(jnp.int32, (UW, NG * UW), 0) // N_A
                  == (lax.broadcasted_iota(jnp.int32, (UW, NG * UW), 1) % UW) // N_A)

    @pl.when(c == 0)
    def _():
        prev_scr[...] = sh0_ref[...]
        for gi, grp in enumerate(groups):
            slabs = [jnp.concatenate([s0_ref[bb, :, u * UW:(u + 1) * UW]] * HU, axis=0) for bb, u in grp]
            g_scr[gi] = jnp.where(state_mask, jnp.concatenate(slabs, axis=1), 0.0)

    pa = pa_ref[...].reshape(rows, A_COLS)
    prevm_scr[...] = pltpu.roll(pa, 1, 0)
    for bb in range(Bb):
        prevm_scr[bb * Tc:bb * Tc + 1, :] = prev_scr[bb:bb + 1, :]
    for bb in range(Bb):
        prev_scr[bb:bb + 1, :] = pa[(bb + 1) * Tc - 1:(bb + 1) * Tc, :]
    xm = pa + (prevm_scr[...] - pa) * mu_ref[...]

    o1, o2, o3 = A_WIDTH, 2 * A_WIDTH, 3 * A_WIDTH
    r = xm[:, :o1]
    k = xm[:, o1:o2]
    v = xm[:, o2:o3]
    wa = xm[:, o3:o3 + W_LORA + A_LORA]
    gl = xm[:, o3 + W_LORA + A_LORA:]
    lane = lax.broadcasted_iota(jnp.int32, wa.shape, 1)
    wa = jnp.where(lane < W_LORA, jnp.tanh(wa), wa)
    lo = _dot(wa.astype(BF16), wwa_ref[...])
    nz = -(w0_ref[...] + lo[:, :A_WIDTH])
    softplus = jnp.maximum(nz, 0.0) + jnp.log(1.0 + jnp.exp(-jnp.abs(nz)))
    log_w = -jnp.exp(-softplus - 0.5)
    a = _sigmoid(a0_ref[...] + lo[:, A_WIDTH:])
    g = _dot(_sigmoid(gl).astype(BF16), g2_ref[...])

    ones_unit = _head_ones(UW, BF16)

    def segsum(x):
        hi, lo = _split_bf16(x)
        return jnp.concatenate([_dot(hi[:, s * UW:(s + 1) * UW], ones_unit) + _dot(lo[:, s * UW:(s + 1) * UW], ones_unit)
                                for s in range(A_WIDTH // UW)], axis=1)
    kk = k * kk_ref[...]
    kk = kk / jnp.maximum(jnp.sqrt(segsum(kk * kk)), 1e-12)
    k2 = k * (1.0 + (a - 1.0) * ka_ref[...])
    kka = kk * a
    bonus = segsum(r * k2 * rk_ref[...]) * v

    ti = lax.broadcasted_iota(jnp.int32, (rows, rows), 0)
    tj = lax.broadcasted_iota(jnp.int32, (rows, rows), 1)
    same_chunk = (ti // C) == (tj // C)
    lw_hi, lw_lo = _split_bf16(log_w)
    hdot = lambda m: _dot(m.astype(BF16), lw_hi) + _dot(m.astype(BF16), lw_lo)
    cs = hdot(same_chunk & (tj <= ti))
    ctot = hdot(same_chunk)
    e_inv = jnp.exp(-cs)
    e_end = jnp.exp(ctot - cs)
    e_tot = jnp.exp(ctot)
    ah = -(kk * jnp.exp(cs - log_w))
    rh = r * jnp.exp(cs)
    bh = kka * e_inv
    kh = k2 * e_inv
    b_end = kka * e_end
    k_end = k2 * e_end

    tile_mask = ((lax.broadcasted_iota(jnp.int32, (RT, UW), 0) % R) // C
                 == lax.broadcasted_iota(jnp.int32, (RT, UW), 1) // N_A)
    ri = lax.broadcasted_iota(jnp.int32, (RT, RT), 0)
    ci = lax.broadcasted_iota(jnp.int32, (RT, RT), 1)
    same_head = (ri // C) == (ci // C)
    strict = same_head & ((ci % C) < (ri % C))
    incl = same_head & ((ci % C) <= (ri % C))
    lane_unit = (lax.broadcasted_iota(jnp.int32, (NG * 2 * C, NG * UW), 0) // (2 * C)
                 == lax.broadcasted_iota(jnp.int32, (NG * 2 * C, NG * UW), 1) // UW)
    bf = lambda x: x.astype(BF16)
    n_sq = C.bit_length() - 2
    n_chunks = Tc // C

    def tile(xs):
        return jnp.where(tile_mask, jnp.concatenate([x for x in xs for _ in range(HU)], axis=0), 0.0)

    def fold(x, i):
        return functools.reduce(lambda p, q: p + q, [x[i * R + j * C:i * R + (j + 1) * C] for j in range(HU)])

    def spread(x):
        return jnp.where(lane_unit, jnp.concatenate([x] * NG, axis=1), 0.0)

    def where_rows(bb, ch, u):
        r0 = bb * Tc + ch * C
        return slice(r0, r0 + C), slice(u * UW, (u + 1) * UW)

    free = {}
    for ch in range(n_chunks):
        for gi, grp in enumerate(groups):
            at = [where_rows(bb, ch, u) for bb, u in grp]
            pick = lambda arr: [arr[sl, ln] for sl, ln in at]
            ahm, rhm, vm = tile(pick(ah)), tile(pick(rh)), tile(pick(v))
            lr = bf(jnp.concatenate([ahm, rhm], axis=0))
            gb = _dot_nt(lr, bf(tile(pick(bh))))
            gk = _dot_nt(lr, bf(tile(pick(kh))))
            x = jnp.concatenate([ahm, _dot(bf(jnp.where(strict, gk[:RT], 0.0)), bf(vm))], axis=1)
            pw = bf(jnp.where(strict, gb[:RT], 0.0))
            for i in range(n_sq + 1):
                x = x + _dot(pw, bf(x))
                if i < n_sq:
                    pw = bf(_dot(pw, pw))
            rx = _dot(bf(jnp.where(incl, gb[RT:], 0.0)), bf(x))
            yv = rx[:, UW:] + _dot(bf(jnp.where(incl, gk[RT:], 0.0)), bf(vm))
            free[ch, gi] = [(fold(x[:, :UW], i), fold(x[:, UW:], i), rh[sl, ln] + fold(rx[:, :UW], i), fold(yv, i))
                            for i, (sl, ln) in enumerate(at)]

    y_parts = {}
    for gi, grp in enumerate(groups):
        g0 = g_scr[gi]
        for ch in range(n_chunks):
            at = [where_rows(bb, ch, u) for bb, u in grp]
            fr = free[ch, gi]
            lhs = jnp.concatenate([jnp.concatenate([w_rows, r_eff], axis=0) for w_rows, _, r_eff, _ in fr], axis=0)
            uy = _dot_nt(bf(spread(lhs)), bf(g0))
            uv, bk, dec = [], [], []
            for i, ((sl, ln), (_, u_free, _, y_free)) in enumerate(zip(at, fr)):
                base = i * 2 * C
                uv += [uy[base:base + C] + u_free, v[sl, ln]]
                bk += [b_end[sl, ln], k_end[sl, ln]]
                dec.append(e_tot[sl.start:sl.start + 1, ln])
                y_parts[ch, grp[i]] = uy[base + C:base + 2 * C] + y_free
            upd = _dot_tn(bf(jnp.concatenate(uv, axis=0)), bf(spread(jnp.concatenate(bk, axis=0))))
            g0 = g0 * jnp.concatenate(dec, axis=1) + jnp.where(state_mask, upd, 0.0)
        g_scr[gi] = g0

        @pl.when(c == pl.num_programs(1) - 1)
        def _(g0=g0, grp=grp):
            for i, (bb, u) in enumerate(grp):
                g1 = g0[:, i * UW:(i + 1) * UW]
                sout_ref[bb, :, u * UW:(u + 1) * UW] = functools.reduce(
                    lambda p, q: p + q, [g1[j * N_A:(j + 1) * N_A] for j in range(HU)])

    y = jnp.concatenate([jnp.concatenate([y_parts[ch, (bb, 0)], y_parts[ch, (bb, 1)]], axis=1)
                         for bb in range(Bb) for ch in range(n_chunks)], axis=0)
    mu = segsum(y) * (1.0 / N_A)
    yc = y - mu
    var = segsum(yc * yc) * (1.0 / N_A)
    yn = yc * lax.rsqrt(var + RWKV_GN_EPS) * lng_ref[...] + lnb_ref[...]
    ya_ref[...] = ((yn + bonus) * g).reshape(Bb, Tc, A_WIDTH).astype(ya_ref.dtype)


def _wkv(pa, shift0, s0, lw, Bb, Tc, out_dtype):
    B, T, _ = pa.shape
    rows = Bb * Tc
    C = min(WKV_CHUNK, Tc)
    vec = lambda: _const_spec((1, A_WIDTH))
    return pl.pallas_call(
        functools.partial(_wkv_kernel, Bb=Bb, Tc=Tc, C=C),
        grid=(B // Bb, T // Tc),
        in_specs=[pl.BlockSpec((Bb, Tc, A_COLS), lambda b, c: (b, c, 0)),
                  pl.BlockSpec((Bb, A_COLS), lambda b, c: (b, 0)),
                  pl.BlockSpec((Bb, N_A, A_WIDTH), lambda b, c: (b, 0, 0)),
                  _const_spec((1, A_COLS)), vec(), vec(),
                  _const_spec((W_LORA + A_LORA, 2 * A_WIDTH)), _const_spec((G_LORA, A_WIDTH)),
                  vec(), vec(), vec(), vec(), vec()],
        out_specs=[pl.BlockSpec((Bb, Tc, A_WIDTH), lambda b, c: (b, c, 0)),
                   pl.BlockSpec((Bb, N_A, A_WIDTH), lambda b, c: (b, 0, 0))],
        out_shape=[jax.ShapeDtypeStruct((B, T, A_WIDTH), out_dtype),
                   jax.ShapeDtypeStruct((B, N_A, A_WIDTH), F32)],
        scratch_shapes=[pltpu.VMEM((Bb, A_COLS), F32), pltpu.VMEM((rows, A_COLS), F32),
                        pltpu.VMEM((Bb // 2, A_WIDTH // 2, 2 * A_WIDTH), F32)],
        compiler_params=_params(("parallel", "arbitrary")),
        name="rwkv_mix",
    )(pa, shift0, s0, lw["mu_a"], lw["w0"], lw["a0"], lw["w_wa"], lw["g2"], lw["k_k"], lw["k_a"], lw["r_k"],
      lw["lnx_g"], lw["lnx_b"])


def _mla_pre_kernel(pb_ref, cos_ref, sin_ref, qg_ref, kvg_ref, wqn_ref, wqpe_ref, wqps_ref, wuk_ref,
                    qcat_ref, kv_ref, kvb_ref):
    pb = pb_ref[...]
    xq = pb[:, :Q_LORA]
    cq = xq * lax.rsqrt(jnp.mean(xq * xq, -1, keepdims=True) + RMS_EPS) * qg_ref[...]
    cqb = cq.astype(BF16)
    qn = _dot(cqb, wqn_ref[...])
    qlat = _dot(qn.astype(BF16), wuk_ref[...])
    cos = cos_ref[...]
    sin = sin_ref[...]
    for h in range(H_B):
        qpe = _dot(cqb, wqpe_ref[h]) * cos + _dot(cqb, wqps_ref[h]) * sin
        qcat_ref[h, :, :KV_LORA] = qlat[:, h * KV_LORA:(h + 1) * KV_LORA].astype(BF16)
        qcat_ref[h, :, KV_LORA:] = qpe.astype(BF16)
    xkv = pb[:, Q_LORA:Q_LORA + KV_LORA]
    ckv = xkv * lax.rsqrt(jnp.mean(xkv * xkv, -1, keepdims=True) + RMS_EPS) * kvg_ref[...]
    kpe = pb[:, Q_LORA + KV_LORA:B_COLS] * cos + pb[:, 512:512 + ROPE_B] * sin
    kv_ref[:, :KV_LORA] = ckv
    kv_ref[:, KV_LORA:] = kpe
    kvb_ref[:, :KV_LORA] = ckv.astype(BF16)
    kvb_ref[:, KV_LORA:] = kpe.astype(BF16)


def _mla_pre(pb, cos, sin, lw):
    n = pb.shape[0]
    tm = _pow2_tile(n, 256)
    row = lambda w: pl.BlockSpec((tm, w), lambda i: (i, 0))
    return pl.pallas_call(
        _mla_pre_kernel,
        grid=(n // tm,),
        in_specs=[row(B_PAD), row(ROPE_B), row(ROPE_B), _const_spec((1, Q_LORA)), _const_spec((1, KV_LORA)),
                  _const_spec((Q_LORA, H_B * NOPE_B)), _const_spec((H_B, Q_LORA, ROPE_B)),
                  _const_spec((H_B, Q_LORA, ROPE_B)), _const_spec((H_B * NOPE_B, H_B * KV_LORA))],
        out_specs=[pl.BlockSpec((H_B, tm, KV_DIM), lambda i: (0, i, 0)), row(KV_DIM), row(KV_DIM)],
        out_shape=[jax.ShapeDtypeStruct((H_B, n, KV_DIM), BF16), jax.ShapeDtypeStruct((n, KV_DIM), F32),
                   jax.ShapeDtypeStruct((n, KV_DIM), BF16)],
        compiler_params=_params(("parallel",)),
        name="mla_pre",
    )(pb, cos, sin, lw["q_norm_g"], lw["kv_norm_g"], lw["w_qn"], lw["w_qpe"], lw["w_qpe_sw"], lw["w_uk_bd"])


def _softmax_step(carry, s, vals):
    m, l, acc = carry
    m_new = jnp.maximum(m, jnp.max(s, axis=-1, keepdims=True))
    p = jnp.exp(s - m_new)
    alpha = jnp.exp(m - m_new)
    l = alpha * l + jnp.sum(p, axis=-1, keepdims=True)
    acc = alpha * acc + _dot(p.astype(vals.dtype), vals)
    return m_new, l, acc


def _flash_kernel(q_ref, kv_ref, o_ref, *, BQ, BK):
    i = pl.program_id(1)
    nfull = (i * BQ) // BK
    c2 = MLA_SCALE * math.log2(math.e)

    rows = H_B * BQ
    q = q_ref[...].reshape(rows, KV_DIM)

    def block(j, carry, masked):
        m, l, acc = carry
        kvj = kv_ref[pl.ds(pl.multiple_of(j * BK, BK), BK), :]
        s = _dot_nt(kvj, q)
        if masked:
            kpos = j * BK + lax.broadcasted_iota(jnp.int32, (BK, rows), 0)
            qpos = i * BQ + (lax.broadcasted_iota(jnp.int32, (BK, rows), 1) % BQ)
            s = jnp.where(kpos <= qpos, s, -jnp.inf)
        m_new = jnp.maximum(m, jnp.max(s, axis=0, keepdims=True) * c2)
        p = jnp.exp2(s * c2 - m_new)
        alpha = jnp.exp2(m - m_new)
        l = alpha * l + jnp.sum(p, axis=0, keepdims=True)
        acc = alpha * acc + _dot_tn(kvj[:, :KV_LORA], p.astype(BF16))
        return m_new, l, acc

    init = (jnp.full((1, rows), -jnp.inf, F32), jnp.zeros((1, rows), F32), jnp.zeros((KV_LORA, rows), F32))
    carry = lax.fori_loop(0, nfull // 2, lambda jj, c: block(2 * jj + 1, block(2 * jj, c, False), False), init)
    carry = lax.cond(nfull % 2 == 1, lambda c: block(nfull - 1, c, False), lambda c: c, carry)
    m, l, acc = block(nfull, carry, True)
    o_t = acc / l
    for h in range(H_B):
        o_ref[:, h * KV_LORA:(h + 1) * KV_LORA] = o_t[:, h * BQ:(h + 1) * BQ].T


def _flash_prompt(qcat, kvb, B, T):
    BQ = min(128, T)
    BK = min(512, T)
    nq = T // BQ
    return pl.pallas_call(
        functools.partial(_flash_kernel, BQ=BQ, BK=BK),
        grid=(B, nq),
        in_specs=[pl.BlockSpec((H_B, BQ, KV_DIM), lambda b, i: (0, b * nq + i, 0)),
                  pl.BlockSpec((T, KV_DIM), lambda b, i: (b, 0))],
        out_specs=pl.BlockSpec((BQ, H_B * KV_LORA), lambda b, i: (b * nq + i, 0)),
        out_shape=jax.ShapeDtypeStruct((B * T, H_B * KV_LORA), F32),
        compiler_params=_params(("parallel", "arbitrary")),
        name="mla_prompt_attn",
    )(qcat, kvb)


def _sample_attn_kernel(ptc_ref, ptn_ref, q_ref, kvn_ref, cache_ref, o_ref, buf, kvb_scr, s_scr, sem,
                        *, layer, B, Ts, PG, npages):
    b = pl.program_id(0)
    slot = lax.rem(b, 2)
    rows = H_B * Ts
    width = PG * PAGE_SIZE
    nchunks = npages // PG
    lanes = 128

    def page_copy(page, sl, p):
        return pltpu.make_async_copy(cache_ref.at[layer, page], buf.at[sl, p], sem.at[sl])

    def start(pt_ref, sl):
        for p in range(npages):
            page_copy(pt_ref[0, 0, p], sl, p).start()

    @pl.when(b == 0)
    def _():
        start(ptc_ref, 0)

    @pl.when(b + 1 < B)
    def _():
        start(ptn_ref, 1 - slot)

    for p in range(npages):
        page_copy(0, slot, p).wait()

    q32 = q_ref[...].reshape(rows, KV_DIM)
    qb = q32.astype(BF16)
    mx = jnp.full((rows, lanes), -jnp.inf, F32)
    for c in range(nchunks):
        kvt = jnp.concatenate([buf[slot, c * PG + p].astype(BF16) for p in range(PG)], axis=1)
        kvb_scr[:, c * width:(c + 1) * width] = kvt
        s = _dot(qb, kvt) * MLA_SCALE
        s_scr[:, c * width:(c + 1) * width] = s
        for j in range(width // lanes):
            mx = jnp.maximum(mx, s[:, j * lanes:(j + 1) * lanes])
    kvn = kvn_ref[...]
    s_new = _dot_nt(q32, kvn) * MLA_SCALE
    tq = lax.broadcasted_iota(jnp.int32, (rows, Ts), 0) % Ts
    tk = lax.broadcasted_iota(jnp.int32, (rows, Ts), 1)
    s_new = jnp.where(tk <= tq, s_new, -jnp.inf)
    m = jnp.maximum(jnp.max(mx, axis=-1, keepdims=True), jnp.max(s_new, axis=-1, keepdims=True))
    lsum = jnp.zeros((rows, lanes), F32)
    acc = jnp.zeros((rows, KV_LORA), F32)
    for c in range(nchunks):
        p = jnp.exp(s_scr[:, c * width:(c + 1) * width] - m)
        for j in range(width // lanes):
            lsum = lsum + p[:, j * lanes:(j + 1) * lanes]
        acc = acc + _dot_nt(p.astype(BF16), kvb_scr[:KV_LORA, c * width:(c + 1) * width])
    p_new = jnp.exp(s_new - m)
    l = jnp.sum(lsum, axis=-1, keepdims=True) + jnp.sum(p_new, axis=-1, keepdims=True)
    o = (acc + _dot(p_new, kvn[:, :KV_LORA])) / l
    for h in range(H_B):
        o_ref[:, h * KV_LORA:(h + 1) * KV_LORA] = o[h * Ts:(h + 1) * Ts]


def _sample_attn(qcat32, kv_new, cache_t, page_table, layer, B, Ts):
    npages = page_table.shape[1]
    PG = _pow2_tile(npages, 16)
    past = npages * PAGE_SIZE
    pt = page_table.reshape(B, 1, npages)
    pt_spec = lambda f: pl.BlockSpec((1, 1, npages), f, memory_space=pltpu.SMEM)
    return pl.pallas_call(
        functools.partial(_sample_attn_kernel, layer=layer, B=B, Ts=Ts, PG=PG, npages=npages),
        grid=(B,),
        in_specs=[pt_spec(lambda b: (b, 0, 0)), pt_spec(lambda b: (jnp.minimum(b + 1, B - 1), 0, 0)),
                  pl.BlockSpec((H_B, Ts, KV_DIM), lambda b: (0, b, 0)),
                  pl.BlockSpec((Ts, KV_DIM), lambda b: (b, 0)),
                  pl.BlockSpec(memory_space=pl.ANY)],
        out_specs=pl.BlockSpec((Ts, H_B * KV_LORA), lambda b: (b, 0)),
        out_shape=jax.ShapeDtypeStruct((B * Ts, H_B * KV_LORA), F32),
        scratch_shapes=[pltpu.VMEM((2, npages, KV_DIM, PAGE_SIZE), F32), pltpu.VMEM((KV_DIM, past), BF16),
                        pltpu.VMEM((H_B * Ts, past), F32), pltpu.SemaphoreType.DMA((2,))],
        compiler_params=_params(("arbitrary",)),
        name="mla_sample_attn",
    )(pt, pt, qcat32, kv_new, cache_t)


def _ret_kernel(pc_ref, cos_ref, sin_ref, r0_ref, gng_ref, gnb_ref, yc_ref, rout_ref, *, Bb, C):
    c = pl.program_id(1)

    @pl.when(c == 0)
    def _():
        rout_ref[...] = r0_ref[...]

    mm = BF16 if C >= 16 else F32
    pc = pc_ref[...]
    cos = cos_ref[...]
    sin = sin_ref[...]
    qk = H_C * DK_C
    vw = H_C * DV_C
    diff = (lax.broadcasted_iota(jnp.int32, (C, C), 0) - lax.broadcasted_iota(jnp.int32, (C, C), 1)).astype(F32)
    row_v = lax.broadcasted_iota(jnp.int32, (C, DV_C), 0).astype(F32)
    row_k = lax.broadcasted_iota(jnp.int32, (C, DK_C), 0).astype(F32)
    for h in range(H_C):
        lg = RET_LOG_GAMMA[h]
        dmat = jnp.where(diff >= 0, jnp.exp(lg * jnp.maximum(diff, 0.0)), 0.0)
        row_decay = jnp.exp((row_v + 1.0) * lg)
        k_decay = jnp.exp((C - 1.0 - row_k) * lg)
        chunk_decay = math.exp(C * lg)
        qh = pc[:, h * DK_C:(h + 1) * DK_C]
        kh = pc[:, qk + h * DK_C:qk + (h + 1) * DK_C]
        q = qh * cos + pltpu.roll(qh, DK_C // 2, 1) * sin
        k = (kh * cos + pltpu.roll(kh, DK_C // 2, 1) * sin) * DK_C ** -0.5
        v = pc[:, 2 * qk + h * DV_C:2 * qk + (h + 1) * DV_C]
        g = pc[:, 2 * qk + vw + h * DV_C:2 * qk + vw + (h + 1) * DV_C]
        for bb in range(Bb):
            sl = slice(bb * C, (bb + 1) * C)
            qb = q[sl].astype(mm)
            kb = k[sl]
            vb = v[sl].astype(mm)
            att = _dot_nt(qb, kb.astype(mm)) * dmat
            state = rout_ref[bb, h]
            o = _dot(att.astype(mm), vb) + _dot(qb, state.astype(mm)) * row_decay
            rout_ref[bb, h] = state * chunk_decay + _dot_tn((kb * k_decay).astype(mm), vb)
            mu = jnp.mean(o, -1, keepdims=True)
            oc = o - mu
            var = jnp.mean(oc * oc, -1, keepdims=True)
            on = (oc * lax.rsqrt(var + RET_GN_EPS) * gng_ref[:, h * DV_C:(h + 1) * DV_C]
                  + gnb_ref[:, h * DV_C:(h + 1) * DV_C])
            gg = g[sl]
            yc_ref[sl, h * DV_C:(h + 1) * DV_C] = on * (gg * _sigmoid(gg))


def _retention(pc, cos, sin, r0, lw, B, T, Bb):
    C = min(RET_CHUNK, T)
    nc = T // C
    rows = Bb * C
    row = lambda w: pl.BlockSpec((rows, w), lambda b, c: (b * nc + c, 0))
    st = pl.BlockSpec((Bb, H_C, DK_C, DV_C), lambda b, c: (b, 0, 0, 0))
    return pl.pallas_call(
        functools.partial(_ret_kernel, Bb=Bb, C=C),
        grid=(B // Bb, nc),
        in_specs=[row(C_COLS), row(DK_C), row(DK_C), st, _const_spec((1, H_C * DV_C)), _const_spec((1, H_C * DV_C))],
        out_specs=[row(H_C * DV_C), st],
        out_shape=[jax.ShapeDtypeStruct((B * T, H_C * DV_C), F32), jax.ShapeDtypeStruct(r0.shape, F32)],
        compiler_params=_params(("parallel", "arbitrary")),
        name="retention",
    )(pc, cos, sin, r0, lw["gn_c_g"], lw["gn_c_b"])


def _merge_kernel(h_ref, ya_ref, ol_ref, yc_ref, wg_ref, woa_ref, wuv_ref, wob_ref, woc_ref, wout_ref,
                  g_ref, b_ref, o_ref):
    h = h_ref[...]
    gate = _sigmoid(_dot(h.astype(BF16), wg_ref[...]))
    ya = _dot(ya_ref[...].astype(BF16), woa_ref[...])
    ob = _dot(ol_ref[...].astype(BF16), wuv_ref[...])
    yb = _dot(ob.astype(BF16), wob_ref[...])
    yc = _dot(yc_ref[...].astype(BF16), woc_ref[...])
    d = D_MODEL
    merged = gate[:, :d] * ya + gate[:, d:2 * d] * yb + gate[:, 2 * d:] * yc
    mix = _dot(merged.astype(BF16), wout_ref[...])
    o_ref[...] = _ln_rows(DN_ALPHA * h + mix, g_ref[...], b_ref[...], LN_EPS)


def _merge(h, ya, olat, yc, lw):
    n, d = h.shape
    tm = _pow2_tile(n, 256)
    row = lambda w: pl.BlockSpec((tm, w), lambda i: (i, 0))
    ws = [lw["w_gate"], lw["wo_a"], lw["w_uv_bd"], lw["wo_b"], lw["wo_c"], lw["w_out"], lw["ln1_g"], lw["ln1_b"]]
    return pl.pallas_call(
        _merge_kernel,
        grid=(n // tm,),
        in_specs=[row(d), row(A_WIDTH), row(H_B * KV_LORA), row(H_C * DV_C)] + [_const_spec(w.shape) for w in ws],
        out_specs=row(d),
        out_shape=jax.ShapeDtypeStruct((n, d), F32),
        compiler_params=_params(("parallel",)),
        name="merge_ln1",
    )(h, ya, olat, yc, *ws)


def _top2_sum(a, b, c, d):
    m_ab, n_ab = jnp.maximum(a, b), jnp.minimum(a, b)
    m_cd, n_cd = jnp.maximum(c, d), jnp.minimum(c, d)
    return jnp.maximum(m_ab, m_cd) + jnp.maximum(jnp.minimum(m_ab, m_cd), jnp.maximum(n_ab, n_cd))


def _ffn_kernel(h_ref, p_ref, rwt_ref, rb_ref, wpg_ref, wp_ref, weg_ref, weu_ref, wed_ref, wsg_ref, wsu_ref,
                wsd_ref, exp_ref, g_ref, b_ref, o_ref):
    h = h_ref[...]
    hb = h.astype(BF16)
    tm = h.shape[0]
    ple = _sigmoid(_dot(hb, wpg_ref[...])) * _dot(p_ref[...].astype(BF16), wp_ref[...])

    logits = lax.dot_general(rwt_ref[...], h, (((1,), (1,)), ((), ())), precision=lax.Precision.HIGHEST,
                             preferred_element_type=F32)
    score = _sigmoid(logits)
    sel = score + rb_ref[...]
    srow = [sel[e:e + 1] for e in range(N_EXPERTS)]
    grp_score = [_top2_sum(*srow[E_PER_GROUP * g:E_PER_GROUP * (g + 1)]) for g in range(N_GROUPS)]
    best = grp_score[0]
    gidx = jnp.zeros((1, tm), jnp.int32)
    for g in range(1, N_GROUPS):
        upd = grp_score[g] > best
        gidx = jnp.where(upd, g, gidx)
        best = jnp.where(upd, grp_score[g], best)
    wrow = []
    for e in range(N_EXPERTS):
        g = e // E_PER_GROUP
        rank = jnp.zeros((1, tm), jnp.int32)
        for e2 in range(E_PER_GROUP * g, E_PER_GROUP * (g + 1)):
            if e2 < e:
                rank += (srow[e2] >= srow[e]).astype(jnp.int32)
            elif e2 > e:
                rank += (srow[e2] > srow[e]).astype(jnp.int32)
        chosen = (gidx == g) & (rank < 2)
        wrow.append(jnp.where(chosen, score[e:e + 1], 0.0))
    wsum = wrow[0]
    for e in range(1, N_EXPERTS):
        wsum = wsum + wrow[e]
    gates = jnp.concatenate(wrow, axis=0) / wsum
    gt = gates.T
    g_hi, g_lo = _split_bf16(gt)
    cw = E_PER_GROUP * D_FF_E

    def group_ffn(x, w_hi, w_lo, g):
        cs = slice(g * cw, (g + 1) * cw)
        hg = _dot(x, weg_ref[:, cs])
        hu = _dot(x, weu_ref[:, cs])
        gfull = _dot(w_hi, exp_ref[:, cs]) + _dot(w_lo, exp_ref[:, cs])
        act = (hg * _sigmoid(hg)) * hu * gfull
        return _dot(act.astype(BF16), wed_ref[cs, :])

    def dense_routed():
        routed = jnp.zeros((tm, D_MODEL), F32)
        for g in range(N_GROUPS):
            routed = routed + group_ffn(hb, g_hi, g_lo, g)
        return routed

    slot = -(-(tm // N_GROUPS) * MOE_SLOT_NUM // (8 * 16)) * 16
    member =jnp.concatenate([(gidx == g).astype(F32) for g in range(N_GROUPS)], axis=0)
    earlier = (lax.broadcasted_iota(jnp.int32, (tm, tm), 0)
               < lax.broadcasted_iota(jnp.int32, (tm, tm), 1)).astype(BF16)
    before = _dot(member.astype(BF16), earlier)
    rank = jnp.sum(member * before, axis=0, keepdims=True)
    dest = gidx.astype(F32) * slot + rank
    perm = (lax.broadcasted_iota(jnp.int32, (N_GROUPS * slot, tm), 0).astype(F32) == dest).astype(BF16)

    def sorted_routed():
        hs = _dot(perm, hb).astype(BF16)
        ws_hi, ws_lo = _split_bf16(_dot(perm, g_hi) + _dot(perm, g_lo))
        outs = [group_ffn(hs[g * slot:(g + 1) * slot], ws_hi[g * slot:(g + 1) * slot],
                          ws_lo[g * slot:(g + 1) * slot], g) for g in range(N_GROUPS)]
        o_hi, o_lo = _split_bf16(jnp.concatenate(outs, axis=0))
        return _dot_tn(perm, o_hi) + _dot_tn(perm, o_lo)

    if tm > slot:
        routed = lax.cond(jnp.max(rank) >= slot, dense_routed, sorted_routed)
    else:
        routed = dense_routed()
    sg = _dot(hb, wsg_ref[...])
    shared = _dot(((sg * _sigmoid(sg)) * _dot(hb, wsu_ref[...])).astype(BF16), wsd_ref[...])
    o_ref[...] = _ln_rows(DN_ALPHA * h + (routed + shared) + ple, g_ref[...], b_ref[...], LN_EPS)


def _ffn(h, p, lw, shared):
    n, d = h.shape
    tm = _pow2_tile(n, 512)
    row = lambda w: pl.BlockSpec((tm, w), lambda i: (i, 0))
    ws = [shared["router_wt"], shared["router_b"], lw["w_ple_gate"], lw["w_ple"], lw["we_gate"], lw["we_up"],
          lw["we_down"], lw["ws_gate"], lw["ws_up"], lw["ws_down"], shared["expand"], lw["ln2_g"], lw["ln2_b"]]
    return pl.pallas_call(
        _ffn_kernel,
        grid=(n // tm,),
        in_specs=[row(d), row(PLE_DIM)] + [_const_spec(w.shape) for w in ws],
        out_specs=row(d),
        out_shape=jax.ShapeDtypeStruct((n, d), F32),
        compiler_params=_params(("parallel",)),
        name="ple_moe_ln2",
    )(h, p, *ws)


def _block_diag(blocks):
    nb = len(blocks)
    r, c = blocks[0].shape
    out = jnp.zeros((nb * r, nb * c), blocks[0].dtype)
    for i, b in enumerate(blocks):
        out = out.at[i * r:(i + 1) * r, i * c:(i + 1) * c].set(b)
    return out


def _layer_weights(i, w_in, mu_a, w0, w2, a0, a2, g2, k_k, k_a, r_k, lnx_g, lnx_b, wo_a, q_norm_g, w_qb, kv_norm_g,
                   w_uk, w_uv, wo_b, gn_c_g, gn_c_b, wo_c, w_out, ln1_g, ln1_b, we_gate, we_up, we_down, ws_gate,
                   ws_up, ws_down, w_ple_gate, w_ple, ln2_g, ln2_b):
    win = w_in[i]
    o0, o1, o2 = A_COLS, A_COLS + B_COLS, A_COLS + B_COLS + C_COLS
    wb = win[:, o0:o1]
    half = ROPE_B // 2
    kpe0 = Q_LORA + KV_LORA
    wb_pad = jnp.zeros((D_MODEL, B_PAD), F32).at[:, :B_COLS].set(wb)
    wb_pad = wb_pad.at[:, 512:512 + half].set(wb[:, kpe0 + half:kpe0 + ROPE_B])
    wb_pad = wb_pad.at[:, 512 + half:512 + ROPE_B].set(wb[:, kpe0:kpe0 + half])
    zeros = jnp.zeros((W_LORA, A_WIDTH), F32)
    w_wa = jnp.concatenate([jnp.concatenate([w2[i], zeros], 1), jnp.concatenate([zeros, a2[i]], 1)], 0)
    wq = w_qb[i].reshape(Q_LORA, H_B, NOPE_B + ROPE_B)
    wq_pe = wq[:, :, NOPE_B:].transpose(1, 0, 2)
    wq_pe_sw = jnp.concatenate([wq_pe[..., half:], wq_pe[..., :half]], -1)
    vec = lambda x: x.reshape(1, -1).astype(F32)
    return dict(
        w_a=win[:, :o0].astype(BF16), w_b=wb_pad.astype(BF16), w_c=win[:, o1:o2].astype(BF16),
        w_gate=win[:, o2:].astype(BF16),
        mu_a=vec(mu_a[i]), w0=vec(w0[i]), a0=vec(a0[i]), w_wa=w_wa.astype(BF16), g2=g2[i].astype(BF16),
        k_k=vec(k_k[i]), k_a=vec(k_a[i]), r_k=vec(r_k[i]), lnx_g=vec(lnx_g[i]), lnx_b=vec(lnx_b[i]),
        wo_a=wo_a[i].astype(BF16),
        q_norm_g=vec(q_norm_g[i]), kv_norm_g=vec(kv_norm_g[i]),
        w_qn=wq[:, :, :NOPE_B].reshape(Q_LORA, H_B * NOPE_B).astype(BF16),
        w_qpe=wq_pe.astype(BF16), w_qpe_sw=wq_pe_sw.astype(BF16),
        w_uk_bd=_block_diag([w_uk[i][:, h, :].T for h in range(H_B)]).astype(BF16),
        w_uv_bd=_block_diag([w_uv[i][:, h, :] for h in range(H_B)]).astype(BF16),
        wo_b=wo_b[i].astype(BF16),
        gn_c_g=vec(gn_c_g[i]), gn_c_b=vec(gn_c_b[i]), wo_c=wo_c[i].astype(BF16),
        w_out=w_out[i].astype(BF16), ln1_g=vec(ln1_g[i]), ln1_b=vec(ln1_b[i]),
        we_gate=we_gate[i].transpose(1, 0, 2).reshape(D_MODEL, N_EXPERTS * D_FF_E).astype(BF16),
        we_up=we_up[i].transpose(1, 0, 2).reshape(D_MODEL, N_EXPERTS * D_FF_E).astype(BF16),
        we_down=we_down[i].reshape(N_EXPERTS * D_FF_E, D_MODEL).astype(BF16),
        ws_gate=ws_gate[i].astype(BF16), ws_up=ws_up[i].astype(BF16), ws_down=ws_down[i].astype(BF16),
        w_ple_gate=w_ple_gate[i].astype(BF16), w_ple=w_ple[i].astype(BF16),
        ln2_g=vec(ln2_g[i]), ln2_b=vec(ln2_b[i]),
    )


def _rope_tables(pos, half):
    inv = ROPE_THETA ** (-jnp.arange(half, dtype=F32) / half)
    ang = pos.astype(F32)[:, None] * inv[None, :]
    cos, sin = jnp.cos(ang), jnp.sin(ang)
    return jnp.concatenate([cos, cos], -1), jnp.concatenate([-sin, sin], -1)


def _wkv_to_lanes(s):
    B = s.shape[0]
    return s.transpose(0, 2, 1, 3).reshape(B, N_A, A_WIDTH)


def _wkv_from_lanes(s):
    B = s.shape[0]
    return s.reshape(B, N_A, H_A, N_A).transpose(0, 2, 1, 3)


def _trunk_layer(h, p, tabs, shift0, wkv0, ret0, attend, lw, shared, B, T, wkv_blk, ret_bb):
    pa, pb, pc = _proj(h, lw["w_a"], lw["w_b"], lw["w_c"])
    ya, wkv = _wkv(pa.reshape(B, T, A_COLS), shift0, wkv0, lw, wkv_blk[0], wkv_blk[1], F32)
    qcat, kv_row, kv_bf = _mla_pre(pb, tabs["cos_b"], tabs["sin_b"], lw)
    olat = attend(qcat, kv_row, kv_bf)
    yc, ret = _retention(pc, tabs["cos_c"], tabs["sin_c"], ret0, lw, B, T, ret_bb)
    h1 = _merge(h, ya.reshape(B * T, A_WIDTH), olat, yc, lw)
    h2 = _ffn(h1, p, lw, shared)
    shift = pa.reshape(B, T, A_COLS)[:, -1]
    return h2, kv_row.reshape(B, T, KV_DIM), wkv, shift, ret


def kernel(x_prompt, x_sample, cache_mla, state_wkv, state_shift, state_ret, page_table, p_prompt, p_sample,
           ln_emb_g, ln_emb_b, w_in, mu_a, w0, w2, a0, a2, g2, k_k, k_a, r_k, lnx_g, lnx_b, wo_a,
           q_norm_g, w_qb, kv_norm_g, w_uk, w_uv, wo_b, gn_c_g, gn_c_b, wo_c, w_out, ln1_g, ln1_b,
           router_w, router_b, we_gate, we_up, we_down, ws_gate, ws_up, ws_down, w_ple_gate, w_ple, ln2_g, ln2_b):
    Bp, Sp, D = x_prompt.shape
    Bs, Ts, _ = x_sample.shape
    past_len = page_table.shape[1] * PAGE_SIZE

    pos_p = jnp.tile(jnp.arange(Sp, dtype=jnp.int32), Bp)
    pos_s = jnp.tile(past_len + jnp.arange(Ts, dtype=jnp.int32), Bs)
    tabs = []
    for pos in (pos_p, pos_s):
        cb, sb = _rope_tables(pos, ROPE_B // 2)
        cc, sc = _rope_tables(pos, DK_C // 2)
        tabs.append(dict(cos_b=cb, sin_b=sb, cos_c=cc, sin_c=sc))
    expand = (jnp.arange(N_EXPERTS)[:, None] == (jnp.arange(N_EXPERTS * D_FF_E)[None, :] // D_FF_E)).astype(BF16)
    shared = dict(router_wt=router_w.T.astype(F32), router_b=router_b.reshape(N_EXPERTS, 1).astype(F32),
                  expand=expand)

    cache_t = jnp.swapaxes(cache_mla, 2, 3)

    hp = _layer_norm(x_prompt.reshape(Bp * Sp, D), ln_emb_g, ln_emb_b)
    hs = _layer_norm(x_sample.reshape(Bs * Ts, D), ln_emb_g, ln_emb_b)
    zero_shift = jnp.zeros((Bp, A_COLS), F32)
    zero_wkv = jnp.zeros((Bp, N_A, A_WIDTH), F32)
    zero_ret = jnp.zeros((Bp, H_C, DK_C, DV_C), F32)
    wkv_tc = min(64, Sp)
    samp_bb = 8 if Bs % 8 == 0 else Bs

    outs = [[] for _ in range(8)]
    for i in range(DEPTH):
        lw = _layer_weights(i, w_in, mu_a, w0, w2, a0, a2, g2, k_k, k_a, r_k, lnx_g, lnx_b, wo_a, q_norm_g, w_qb,
                            kv_norm_g, w_uk, w_uv, wo_b, gn_c_g, gn_c_b, wo_c, w_out, ln1_g, ln1_b, we_gate, we_up,
                            we_down, ws_gate, ws_up, ws_down, w_ple_gate, w_ple, ln2_g, ln2_b)
        attend_p = lambda qcat, kv_row, kv_bf: _flash_prompt(qcat, kv_bf, Bp, Sp)
        hp, kv_p, wkv_p, sh_p, ret_p = _trunk_layer(
            hp, p_prompt[i].reshape(Bp * Sp, PLE_DIM), tabs[0], zero_shift, zero_wkv, zero_ret, attend_p, lw, shared,
            Bp, Sp, (Bp, wkv_tc), 1)
        attend_s = lambda qcat, kv_row, kv_bf, i=i: _sample_attn(qcat.astype(F32), kv_row, cache_t, page_table, i,
                                                                 Bs, Ts)
        hs, kv_s, wkv_s, sh_s, ret_s = _trunk_layer(
            hs, p_sample[i].reshape(Bs * Ts, PLE_DIM), tabs[1], state_shift[i], _wkv_to_lanes(state_wkv[i]),
            state_ret[i], attend_s, lw, shared, Bs, Ts, (samp_bb, Ts), samp_bb)
        for lst, val in zip(outs, (kv_p, kv_s, _wkv_from_lanes(wkv_p), _wkv_from_lanes(wkv_s), sh_p, sh_s,
                                   ret_p, ret_s)):
            lst.append(val)
    return (hp.reshape(Bp, Sp, D), hs.reshape(Bs, Ts, D)) + tuple(jnp.stack(l) for l in outs)
```

```python
import functools
import math

import jax
import jax.numpy as jnp
from jax import lax
from jax.experimental import pallas as pl
from jax.experimental.pallas import tpu as pltpu

F32 = jnp.float32
BF16 = jnp.bfloat16

D_MODEL = 1024
H_A, N_A = 8, 64
A_WIDTH = H_A * N_A
W_LORA, A_LORA, G_LORA = 64, 64, 128
A_COLS = 3 * A_WIDTH + W_LORA + A_LORA + G_LORA
RWKV_GN_EPS = 64e-5
WKV_CHUNK = 16
H_B, Q_LORA, KV_LORA, NOPE_B, ROPE_B, DV_B = 8, 256, 128, 64, 32, 64
KV_DIM = KV_LORA + ROPE_B
B_COLS = Q_LORA + KV_LORA + ROPE_B
B_PAD = 640
MLA_SCALE = (NOPE_B + ROPE_B) ** -0.5
H_C, DK_C, DV_C = 4, 128, 256
C_COLS = 2 * H_C * DK_C + 2 * H_C * DV_C
RET_CHUNK = 128
RET_GN_EPS = 1e-5
RET_LOG_GAMMA = tuple(math.log1p(-(2.0 ** (-5.0 - h))) for h in range(H_C))
N_BRANCH = 3
GATE_COLS = N_BRANCH * D_MODEL
ROPE_THETA = 10000.0
N_EXPERTS, N_GROUPS, E_PER_GROUP, D_FF_E, D_FF_SHARED = 16, 4, 4, 256, 256
MOE_SLOT_NUM = 11
PLE_DIM = 256
PAGE_SIZE = 128
DEPTH = 2
DN_ALPHA = (2 * DEPTH) ** 0.25
LN_EPS = 1e-5
RMS_EPS = 1e-6

VMEM_LIMIT = 56 * 1024 * 1024


def _pow2_tile(n, pref):
    t = 1
    while t * 2 <= pref and n % (t * 2) == 0:
        t *= 2
    return t


def _params(sem):
    return pltpu.CompilerParams(dimension_semantics=sem, vmem_limit_bytes=VMEM_LIMIT)


def _const_spec(shape):
    nd = len(shape)
    return pl.BlockSpec(shape, lambda *_: (0,) * nd, pipeline_mode=pl.Buffered(1))


def _sigmoid(x):
    return 1.0 / (1.0 + jnp.exp(-x))


def _ln_rows(x, g, b, eps):
    mu = jnp.mean(x, -1, keepdims=True)
    xc = x - mu
    var = jnp.mean(xc * xc, -1, keepdims=True)
    return xc * lax.rsqrt(var + eps) * g + b


def _dot(a, b):
    return jnp.dot(a, b, preferred_element_type=F32)


def _split_bf16(x):
    hi = x.astype(BF16)
    return hi, (x - hi.astype(F32)).astype(BF16)


def _dot_nt(a, b):
    return lax.dot_general(a, b, (((1,), (1,)), ((), ())), preferred_element_type=F32)


def _dot_tn(a, b):
    return lax.dot_general(a, b, (((0,), (0,)), ((), ())), preferred_element_type=F32)


def _ln_kernel(x_ref, g_ref, b_ref, o_ref):
    o_ref[...] = _ln_rows(x_ref[...], g_ref[...], b_ref[...], LN_EPS)


def _layer_norm(x, g, b):
    n, d = x.shape
    tm = _pow2_tile(n, 512)
    return pl.pallas_call(
        _ln_kernel,
        grid=(n // tm,),
        in_specs=[pl.BlockSpec((tm, d), lambda i: (i, 0)), _const_spec((1, d)), _const_spec((1, d))],
        out_specs=pl.BlockSpec((tm, d), lambda i: (i, 0)),
        out_shape=jax.ShapeDtypeStruct((n, d), F32),
        compiler_params=_params(("parallel",)),
        name="embed_ln",
    )(x, g.reshape(1, d), b.reshape(1, d))


def _proj_kernel(h_ref, wa_ref, wb_ref, wc_ref, pa_ref, pb_ref, pc_ref):
    x = h_ref[...].astype(BF16)
    pa_ref[...] = _dot(x, wa_ref[...])
    pb_ref[...] = _dot(x, wb_ref[...])
    pc_ref[...] = _dot(x, wc_ref[...])


def _proj(h, wa, wb, wc):
    n, d = h.shape
    tm = _pow2_tile(n, 256)
    row = lambda w: pl.BlockSpec((tm, w), lambda i: (i, 0))
    return pl.pallas_call(
        _proj_kernel,
        grid=(n // tm,),
        in_specs=[row(d), _const_spec(wa.shape), _const_spec(wb.shape), _const_spec(wc.shape)],
        out_specs=[row(A_COLS), row(B_PAD), row(C_COLS)],
        out_shape=[jax.ShapeDtypeStruct((n, A_COLS), F32), jax.ShapeDtypeStruct((n, B_PAD), F32),
                   jax.ShapeDtypeStruct((n, C_COLS), F32)],
        compiler_params=_params(("parallel",)),
        name="in_proj",
    )(h, wa, wb, wc)


def _head_ones(n, dtype):
    r = lax.broadcasted_iota(jnp.int32, (n, n), 0) // N_A
    c = lax.broadcasted_iota(jnp.int32, (n, n), 1) // N_A
    return (r == c).astype(dtype)


def _wkv_kernel(pa_ref, sh0_ref, s0_ref, mu_ref, w0_ref, a0_ref, wwa_ref, g2_ref, kk_ref, ka_ref, rk_ref,
                lng_ref, lnb_ref, ya_ref, sout_ref, prev_scr, prevm_scr, g_scr, *, Bb, Tc, C):
    c = pl.program_id(1)
    rows = Bb * Tc
    UW = A_WIDTH // 2
    HU = UW // N_A
    NG = 4
    R = HU * C
    RT = NG * R
    units = [(bb, u) for bb in range(Bb) for u in range(2)]
    groups = [units[i:i + NG] for i in range(0, len(units), NG)]
    state_mask = (lax.broadcasted_iota(jnp.int32, (UW, NG * UW), 0) // N_A
                  == (lax.broadcasted_iota(jnp.int32, (UW, NG * UW), 1) % UW) // N_A)

    @pl.when(c == 0)
    def _():
        prev_scr[...] = sh0_ref[...]
        for gi, grp in enumerate(groups):
            slabs = [jnp.concatenate([s0_ref[bb, :, u * UW:(u + 1) * UW]] * HU, axis=0) for bb, u in grp]
            g_scr[gi] = jnp.where(state_mask, jnp.concatenate(slabs, axis=1), 0.0)

    pa = pa_ref[...].reshape(rows, A_COLS)
    prevm_scr[...] = pltpu.roll(pa, 1, 0)
    for bb in range(Bb):
        prevm_scr[bb * Tc:bb * Tc + 1, :] = prev_scr[bb:bb + 1, :]
    for bb in range(Bb):
        prev_scr[bb:bb + 1, :] = pa[(bb + 1) * Tc - 1:(bb + 1) * Tc, :]
    xm = pa + (prevm_scr[...] - pa) * mu_ref[...]

    o1, o2, o3 = A_WIDTH, 2 * A_WIDTH, 3 * A_WIDTH
    r = xm[:, :o1]
    k = xm[:, o1:o2]
    v = xm[:, o2:o3]
    wa = xm[:, o3:o3 + W_LORA + A_LORA]
    gl = xm[:, o3 + W_LORA + A_LORA:]
    lane = lax.broadcasted_iota(jnp.int32, wa.shape, 1)
    wa = jnp.where(lane < W_LORA, jnp.tanh(wa), wa)
    lo = _dot(wa.astype(BF16), wwa_ref[...])
    nz = -(w0_ref[...] + lo[:, :A_WIDTH])
    softplus = jnp.maximum(nz, 0.0) + jnp.log(1.0 + jnp.exp(-jnp.abs(nz)))
    log_w = -jnp.exp(-softplus - 0.5)
    a = _sigmoid(a0_ref[...] + lo[:, A_WIDTH:])
    g = _dot(_sigmoid(gl).astype(BF16), g2_ref[...])

    ones_unit = _head_ones(UW, BF16)

    def segsum(x):
        hi, lo = _split_bf16(x)
        return jnp.concatenate([_dot(hi[:, s * UW:(s + 1) * UW], ones_unit) + _dot(lo[:, s * UW:(s + 1) * UW], ones_unit)
                                for s in range(A_WIDTH // UW)], axis=1)
    kk = k * kk_ref[...]
    kk = kk / jnp.maximum(jnp.sqrt(segsum(kk * kk)), 1e-12)
    k2 = k * (1.0 + (a - 1.0) * ka_ref[...])
    kka = kk * a
    bonus = segsum(r * k2 * rk_ref[...]) * v

    ti = lax.broadcasted_iota(jnp.int32, (rows, rows), 0)
    tj = lax.broadcasted_iota(jnp.int32, (rows, rows), 1)
    same_chunk = (ti // C) == (tj // C)
    lw_hi, lw_lo = _split_bf16(log_w)
    hdot = lambda m: _dot(m.astype(BF16), lw_hi) + _dot(m.astype(BF16), lw_lo)
    cs = hdot(same_chunk & (tj <= ti))
    ctot = hdot(same_chunk)
    e_inv = jnp.exp(-cs)
    e_end = jnp.exp(ctot - cs)
    e_tot = jnp.exp(ctot)
    ah = -(kk * jnp.exp(cs - log_w))
    rh = r * jnp.exp(cs)
    bh = kka * e_inv
    kh = k2 * e_inv
    b_end = kka * e_end
    k_end = k2 * e_end

    tile_mask = ((lax.broadcasted_iota(jnp.int32, (RT, UW), 0) % R) // C
                 == lax.broadcasted_iota(jnp.int32, (RT, UW), 1) // N_A)
    ri = lax.broadcasted_iota(jnp.int32, (RT, RT), 0)
    ci = lax.broadcasted_iota(jnp.int32, (RT, RT), 1)
    same_head = (ri // C) == (ci // C)
    strict = same_head & ((ci % C) < (ri % C))
    incl = same_head & ((ci % C) <= (ri % C))
    lane_unit = (lax.broadcasted_iota(jnp.int32, (NG * 2 * C, NG * UW), 0) // (2 * C)
                 == lax.broadcasted_iota(jnp.int32, (NG * 2 * C, NG * UW), 1) // UW)
    bf = lambda x: x.astype(BF16)
    n_sq = C.bit_length() - 2
    n_chunks = Tc // C

    def tile(xs):
        return jnp.where(tile_mask, jnp.concatenate([x for x in xs for _ in range(HU)], axis=0), 0.0)

    def fold(x, i):
        return functools.reduce(lambda p, q: p + q, [x[i * R + j * C:i * R + (j + 1) * C] for j in range(HU)])

    def spread(x):
        return jnp.where(lane_unit, jnp.concatenate([x] * NG, axis=1), 0.0)

    def where_rows(bb, ch, u):
        r0 = bb * Tc + ch * C
        return slice(r0, r0 + C), slice(u * UW, (u + 1) * UW)

    free = {}
    for ch in range(n_chunks):
        for gi, grp in enumerate(groups):
            at = [where_rows(bb, ch, u) for bb, u in grp]
            pick = lambda arr: [arr[sl, ln] for sl, ln in at]
            ahm, rhm, vm = tile(pick(ah)), tile(pick(rh)), tile(pick(v))
            lr = bf(jnp.concatenate([ahm, rhm], axis=0))
            gb = _dot_nt(lr, bf(tile(pick(bh))))
            gk = _dot_nt(lr, bf(tile(pick(kh))))
            x = jnp.concatenate([ahm, _dot(bf(jnp.where(strict, gk[:RT], 0.0)), bf(vm))], axis=1)
            pw = bf(jnp.where(strict, gb[:RT], 0.0))
            for i in range(n_sq + 1):
                x = x + _dot(pw, bf(x))
                if i < n_sq:
                    pw = bf(_dot(pw, pw))
            rx = _dot(bf(jnp.where(incl, gb[RT:], 0.0)), bf(x))
            yv = rx[:, UW:] + _dot(bf(jnp.where(incl, gk[RT:], 0.0)), bf(vm))
            free[ch, gi] = [(fold(x[:, :UW], i), fold(x[:, UW:], i), rh[sl, ln] + fold(rx[:, :UW], i), fold(yv, i))
                            for i, (sl, ln) in enumerate(at)]

    y_parts = {}
    for gi, grp in enumerate(groups):
        g0 = g_scr[gi]
        for ch in range(n_chunks):
            at = [where_rows(bb, ch, u) for bb, u in grp]
            fr = free[ch, gi]
            lhs = jnp.concatenate([jnp.concatenate([w_rows, r_eff], axis=0) for w_rows, _, r_eff, _ in fr], axis=0)
            uy = _dot_nt(bf(spread(lhs)), bf(g0))
            uv, bk, dec = [], [], []
            for i, ((sl, ln), (_, u_free, _, y_free)) in enumerate(zip(at, fr)):
                base = i * 2 * C
                uv += [uy[base:base + C] + u_free, v[sl, ln]]
                bk += [b_end[sl, ln], k_end[sl, ln]]
                dec.append(e_tot[sl.start:sl.start + 1, ln])
                y_parts[ch, grp[i]] = uy[base + C:base + 2 * C] + y_free
            upd = _dot_tn(bf(jnp.concatenate(uv, axis=0)), bf(spread(jnp.concatenate(bk, axis=0))))
            g0 = g0 * jnp.concatenate(dec, axis=1) + jnp.where(state_mask, upd, 0.0)
        g_scr[gi] = g0

        @pl.when(c == pl.num_programs(1) - 1)
        def _(g0=g0, grp=grp):
            for i, (bb, u) in enumerate(grp):
                g1 = g0[:, i * UW:(i + 1) * UW]
                sout_ref[bb, :, u * UW:(u + 1) * UW] = functools.reduce(
                    lambda p, q: p + q, [g1[j * N_A:(j + 1) * N_A] for j in range(HU)])

    y = jnp.concatenate([jnp.concatenate([y_parts[ch, (bb, 0)], y_parts[ch, (bb, 1)]], axis=1)
                         for bb in range(Bb) for ch in range(n_chunks)], axis=0)
    mu = segsum(y) * (1.0 / N_A)
    yc = y - mu
    var = segsum(yc * yc) * (1.0 / N_A)
    yn = yc * lax.rsqrt(var + RWKV_GN_EPS) * lng_ref[...] + lnb_ref[...]
    ya_ref[...] = ((yn + bonus) * g).reshape(Bb, Tc, A_WIDTH).astype(ya_ref.dtype)


def _wkv(pa, shift0, s0, lw, Bb, Tc, out_dtype):
    B, T, _ = pa.shape
    rows = Bb * Tc
    C = min(WKV_CHUNK, Tc)
    vec = lambda: _const_spec((1, A_WIDTH))
    return pl.pallas_call(
        functools.partial(_wkv_kernel, Bb=Bb, Tc=Tc, C=C),
        grid=(B // Bb, T // Tc),
        in_specs=[pl.BlockSpec((Bb, Tc, A_COLS), lambda b, c: (b, c, 0)),
                  pl.BlockSpec((Bb, A_COLS), lambda b, c: (b, 0)),
                  pl.BlockSpec((Bb, N_A, A_WIDTH), lambda b, c: (b, 0, 0)),
                  _const_spec((1, A_COLS)), vec(), vec(),
                  _const_spec((W_LORA + A_LORA, 2 * A_WIDTH)), _const_spec((G_LORA, A_WIDTH)),
                  vec(), vec(), vec(), vec(), vec()],
        out_specs=[pl.BlockSpec((Bb, Tc, A_WIDTH), lambda b, c: (b, c, 0)),
                   pl.BlockSpec((Bb, N_A, A_WIDTH), lambda b, c: (b, 0, 0))],
        out_shape=[jax.ShapeDtypeStruct((B, T, A_WIDTH), out_dtype),
                   jax.ShapeDtypeStruct((B, N_A, A_WIDTH), F32)],
        scratch_shapes=[pltpu.VMEM((Bb, A_COLS), F32), pltpu.VMEM((rows, A_COLS), F32),
                        pltpu.VMEM((Bb // 2, A_WIDTH // 2, 2 * A_WIDTH), F32)],
        compiler_params=_params(("parallel", "arbitrary")),
        name="rwkv_mix",
    )(pa, shift0, s0, lw["mu_a"], lw["w0"], lw["a0"], lw["w_wa"], lw["g2"], lw["k_k"], lw["k_a"], lw["r_k"],
      lw["lnx_g"], lw["lnx_b"])


def _mla_pre_kernel(pb_ref, cos_ref, sin_ref, qg_ref, kvg_ref, wqn_ref, wqpe_ref, wqps_ref, wuk_ref,
                    qcat_ref, kv_ref, kvb_ref):
    pb = pb_ref[...]
    xq = pb[:, :Q_LORA]
    cq = xq * lax.rsqrt(jnp.mean(xq * xq, -1, keepdims=True) + RMS_EPS) * qg_ref[...]
    cqb = cq.astype(BF16)
    qn = _dot(cqb, wqn_ref[...])
    qlat = _dot(qn.astype(BF16), wuk_ref[...])
    cos = cos_ref[...]
    sin = sin_ref[...]
    for h in range(H_B):
        qpe = _dot(cqb, wqpe_ref[h]) * cos + _dot(cqb, wqps_ref[h]) * sin
        qcat_ref[h, :, :KV_LORA] = qlat[:, h * KV_LORA:(h + 1) * KV_LORA].astype(BF16)
        qcat_ref[h, :, KV_LORA:] = qpe.astype(BF16)
    xkv = pb[:, Q_LORA:Q_LORA + KV_LORA]
    ckv = xkv * lax.rsqrt(jnp.mean(xkv * xkv, -1, keepdims=True) + RMS_EPS) * kvg_ref[...]
    kpe = pb[:, Q_LORA + KV_LORA:B_COLS] * cos + pb[:, 512:512 + ROPE_B] * sin
    kv_ref[:, :KV_LORA] = ckv
    kv_ref[:, KV_LORA:] = kpe
    kvb_ref[:, :KV_LORA] = ckv.astype(BF16)
    kvb_ref[:, KV_LORA:] = kpe.astype(BF16)


def _mla_pre(pb, cos, sin, lw):
    n = pb.shape[0]
    tm = _pow2_tile(n, 256)
    row = lambda w: pl.BlockSpec((tm, w), lambda i: (i, 0))
    return pl.pallas_call(
        _mla_pre_kernel,
        grid=(n // tm,),
        in_specs=[row(B_PAD), row(ROPE_B), row(ROPE_B), _const_spec((1, Q_LORA)), _const_spec((1, KV_LORA)),
                  _const_spec((Q_LORA, H_B * NOPE_B)), _const_spec((H_B, Q_LORA, ROPE_B)),
                  _const_spec((H_B, Q_LORA, ROPE_B)), _const_spec((H_B * NOPE_B, H_B * KV_LORA))],
        out_specs=[pl.BlockSpec((H_B, tm, KV_DIM), lambda i: (0, i, 0)), row(KV_DIM), row(KV_DIM)],
        out_shape=[jax.ShapeDtypeStruct((H_B, n, KV_DIM), BF16), jax.ShapeDtypeStruct((n, KV_DIM), F32),
                   jax.ShapeDtypeStruct((n, KV_DIM), BF16)],
        compiler_params=_params(("parallel",)),
        name="mla_pre",
    )(pb, cos, sin, lw["q_norm_g"], lw["kv_norm_g"], lw["w_qn"], lw["w_qpe"], lw["w_qpe_sw"], lw["w_uk_bd"])


def _softmax_step(carry, s, vals):
    m, l, acc = carry
    m_new = jnp.maximum(m, jnp.max(s, axis=-1, keepdims=True))
    p = jnp.exp(s - m_new)
    alpha = jnp.exp(m - m_new)
    l = alpha * l + jnp.sum(p, axis=-1, keepdims=True)
    acc = alpha * acc + _dot(p.astype(vals.dtype), vals)
    return m_new, l, acc


def _flash_kernel(q_ref, kv_ref, o_ref, *, BQ, BK):
    i = pl.program_id(1)
    nfull = (i * BQ) // BK
    c2 = MLA_SCALE * math.log2(math.e)

    rows = H_B * BQ
    q = q_ref[...].reshape(rows, KV_DIM)

    def block(j, carry, masked):
        m, l, acc = carry
        kvj = kv_ref[pl.ds(pl.multiple_of(j * BK, BK), BK), :]
        s = _dot_nt(kvj, q)
        if masked:
            kpos = j * BK + lax.broadcasted_iota(jnp.int32, (BK, rows), 0)
            qpos = i * BQ + (lax.broadcasted_iota(jnp.int32, (BK, rows), 1) % BQ)
            s = jnp.where(kpos <= qpos, s, -jnp.inf)
        m_new = jnp.maximum(m, jnp.max(s, axis=0, keepdims=True) * c2)
        p = jnp.exp2(s * c2 - m_new)
        alpha = jnp.exp2(m - m_new)
        l = alpha * l + jnp.sum(p, axis=0, keepdims=True)
        acc = alpha * acc + _dot_tn(kvj[:, :KV_LORA], p.astype(BF16))
        return m_new, l, acc

    init = (jnp.full((1, rows), -jnp.inf, F32), jnp.zeros((1, rows), F32), jnp.zeros((KV_LORA, rows), F32))
    carry = lax.fori_loop(0, nfull // 2, lambda jj, c: block(2 * jj + 1, block(2 * jj, c, False), False), init)
    carry = lax.cond(nfull % 2 == 1, lambda c: block(nfull - 1, c, False), lambda c: c, carry)
    m, l, acc = block(nfull, carry, True)
    o_t = acc / l
    for h in range(H_B):
        o_ref[:, h * KV_LORA:(h + 1) * KV_LORA] = o_t[:, h * BQ:(h + 1) * BQ].T


def _flash_prompt(qcat, kvb, B, T):
    BQ = min(128, T)
    BK = min(512, T)
    nq = T // BQ
    return pl.pallas_call(
        functools.partial(_flash_kernel, BQ=BQ, BK=BK),
        grid=(B, nq),
        in_specs=[pl.BlockSpec((H_B, BQ, KV_DIM), lambda b, i: (0, b * nq + i, 0)),
                  pl.BlockSpec((T, KV_DIM), lambda b, i: (b, 0))],
        out_specs=pl.BlockSpec((BQ, H_B * KV_LORA), lambda b, i: (b * nq + i, 0)),
        out_shape=jax.ShapeDtypeStruct((B * T, H_B * KV_LORA), F32),
        compiler_params=_params(("parallel", "arbitrary")),
        name="mla_prompt_attn",
    )(qcat, kvb)


def _sample_attn_kernel(ptc_ref, ptn_ref, q_ref, kvn_ref, cache_ref, o_ref, buf, kvb_scr, s_scr, sem,
                        *, layer, B, Ts, PG, npages):
    b = pl.program_id(0)
    slot = lax.rem(b, 2)
    rows = H_B * Ts
    width = PG * PAGE_SIZE
    nchunks = npages // PG
    lanes = 128

    def page_copy(page, sl, p):
        return pltpu.make_async_copy(cache_ref.at[layer, page], buf.at[sl, p], sem.at[sl])

    def start(pt_ref, sl):
        for p in range(npages):
            page_copy(pt_ref[0, 0, p], sl, p).start()

    @pl.when(b == 0)
    def _():
        start(ptc_ref, 0)

    @pl.when(b + 1 < B)
    def _():
        start(ptn_ref, 1 - slot)

    for p in range(npages):
        page_copy(0, slot, p).wait()

    q32 = q_ref[...].reshape(rows, KV_DIM)
    qb = q32.astype(BF16)
    mx = jnp.full((rows, lanes), -jnp.inf, F32)
    for c in range(nchunks):
        kvt = jnp.concatenate([buf[slot, c * PG + p].astype(BF16) for p in range(PG)], axis=1)
        kvb_scr[:, c * width:(c + 1) * width] = kvt
        s = _dot(qb, kvt) * MLA_SCALE
        s_scr[:, c * width:(c + 1) * width] = s
        for j in range(width // lanes):
            mx = jnp.maximum(mx, s[:, j * lanes:(j + 1) * lanes])
    kvn = kvn_ref[...]
    s_new = _dot_nt(q32, kvn) * MLA_SCALE
    tq = lax.broadcasted_iota(jnp.int32, (rows, Ts), 0) % Ts
    tk = lax.broadcasted_iota(jnp.int32, (rows, Ts), 1)
    s_new = jnp.where(tk <= tq, s_new, -jnp.inf)
    m = jnp.maximum(jnp.max(mx, axis=-1, keepdims=True), jnp.max(s_new, axis=-1, keepdims=True))
    lsum = jnp.zeros((rows, lanes), F32)
    acc = jnp.zeros((rows, KV_LORA), F32)
    for c in range(nchunks):
        p = jnp.exp(s_scr[:, c * width:(c + 1) * width] - m)
        for j in range(width // lanes):
            lsum = lsum + p[:, j * lanes:(j + 1) * lanes]
        acc = acc + _dot_nt(p.astype(BF16), kvb_scr[:KV_LORA, c * width:(c + 1) * width])
    p_new = jnp.exp(s_new - m)
    l = jnp.sum(lsum, axis=-1, keepdims=True) + jnp.sum(p_new, axis=-1, keepdims=True)
    o = (acc + _dot(p_new, kvn[:, :KV_LORA])) / l
    for h in range(H_B):
        o_ref[:, h * KV_LORA:(h + 1) * KV_LORA] = o[h * Ts:(h + 1) * Ts]


def _sample_attn(qcat32, kv_new, cache_t, page_table, layer, B, Ts):
    npages = page_table.shape[1]
    PG = _pow2_tile(npages, 16)
    past = npages * PAGE_SIZE
    pt = page_table.reshape(B, 1, npages)
    pt_spec = lambda f: pl.BlockSpec((1, 1, npages), f, memory_space=pltpu.SMEM)
    return pl.pallas_call(
        functools.partial(_sample_attn_kernel, layer=layer, B=B, Ts=Ts, PG=PG, npages=npages),
        grid=(B,),
        in_specs=[pt_spec(lambda b: (b, 0, 0)), pt_spec(lambda b: (jnp.minimum(b + 1, B - 1), 0, 0)),
                  pl.BlockSpec((H_B, Ts, KV_DIM), lambda b: (0, b, 0)),
                  pl.BlockSpec((Ts, KV_DIM), lambda b: (b, 0)),
                  pl.BlockSpec(memory_space=pl.ANY)],
        out_specs=pl.BlockSpec((Ts, H_B * KV_LORA), lambda b: (b, 0)),
        out_shape=jax.ShapeDtypeStruct((B * Ts, H_B * KV_LORA), F32),
        scratch_shapes=[pltpu.VMEM((2, npages, KV_DIM, PAGE_SIZE), F32), pltpu.VMEM((KV_DIM, past), BF16),
                        pltpu.VMEM((H_B * Ts, past), F32), pltpu.SemaphoreType.DMA((2,))],
        compiler_params=_params(("arbitrary",)),
        name="mla_sample_attn",
    )(pt, pt, qcat32, kv_new, cache_t)


def _ret_kernel(pc_ref, cos_ref, sin_ref, r0_ref, gng_ref, gnb_ref, yc_ref, rout_ref, *, Bb, C):
    c = pl.program_id(1)

    @pl.when(c == 0)
    def _():
        rout_ref[...] = r0_ref[...]

    mm = BF16 if C >= 16 else F32
    pc = pc_ref[...]
    cos = cos_ref[...]
    sin = sin_ref[...]
    qk = H_C * DK_C
    vw = H_C * DV_C
    diff = (lax.broadcasted_iota(jnp.int32, (C, C), 0) - lax.broadcasted_iota(jnp.int32, (C, C), 1)).astype(F32)
    row_v = lax.broadcasted_iota(jnp.int32, (C, DV_C), 0).astype(F32)
    row_k = lax.broadcasted_iota(jnp.int32, (C, DK_C), 0).astype(F32)
    for h in range(H_C):
        lg = RET_LOG_GAMMA[h]
        dmat = jnp.where(diff >= 0, jnp.exp(lg * jnp.maximum(diff, 0.0)), 0.0)
        row_decay = jnp.exp((row_v + 1.0) * lg)
        k_decay = jnp.exp((C - 1.0 - row_k) * lg)
        chunk_decay = math.exp(C * lg)
        qh = pc[:, h * DK_C:(h + 1) * DK_C]
        kh = pc[:, qk + h * DK_C:qk + (h + 1) * DK_C]
        q = qh * cos + pltpu.roll(qh, DK_C // 2, 1) * sin
        k = (kh * cos + pltpu.roll(kh, DK_C // 2, 1) * sin) * DK_C ** -0.5
        v = pc[:, 2 * qk + h * DV_C:2 * qk + (h + 1) * DV_C]
        g = pc[:, 2 * qk + vw + h * DV_C:2 * qk + vw + (h + 1) * DV_C]
        for bb in range(Bb):
            sl = slice(bb * C, (bb + 1) * C)
            qb = q[sl].astype(mm)
            kb = k[sl]
            vb = v[sl].astype(mm)
            att = _dot_nt(qb, kb.astype(mm)) * dmat
            state = rout_ref[bb, h]
            o = _dot(att.astype(mm), vb) + _dot(qb, state.astype(mm)) * row_decay
            rout_ref[bb, h] = state * chunk_decay + _dot_tn((kb * k_decay).astype(mm), vb)
            mu = jnp.mean(o, -1, keepdims=True)
            oc = o - mu
            var = jnp.mean(oc * oc, -1, keepdims=True)
            on = (oc * lax.rsqrt(var + RET_GN_EPS) * gng_ref[:, h * DV_C:(h + 1) * DV_C]
                  + gnb_ref[:, h * DV_C:(h + 1) * DV_C])
            gg = g[sl]
            yc_ref[sl, h * DV_C:(h + 1) * DV_C] = on * (gg * _sigmoid(gg))


def _retention(pc, cos, sin, r0, lw, B, T, Bb, layer=None):
    C = min(RET_CHUNK, T)
    nc = T // C
    rows = Bb * C
    row = lambda w: pl.BlockSpec((rows, w), lambda b, c: (b * nc + c, 0))
    st = pl.BlockSpec((Bb, H_C, DK_C, DV_C), lambda b, c: (b, 0, 0, 0))
    st_in = st if layer is None else pl.BlockSpec((None, Bb, H_C, DK_C, DV_C), lambda b, c: (layer, b, 0, 0, 0))
    return pl.pallas_call(
        functools.partial(_ret_kernel, Bb=Bb, C=C),
        grid=(B // Bb, nc),
        in_specs=[row(C_COLS), row(DK_C), row(DK_C), st_in, _const_spec((1, H_C * DV_C)),
                  _const_spec((1, H_C * DV_C))],
        out_specs=[row(H_C * DV_C), st],
        out_shape=[jax.ShapeDtypeStruct((B * T, H_C * DV_C), F32),
                   jax.ShapeDtypeStruct((B, H_C, DK_C, DV_C), F32)],
        compiler_params=_params(("parallel", "arbitrary")),
        name="retention",
    )(pc, cos, sin, r0, lw["gn_c_g"], lw["gn_c_b"])


def _merge_kernel(h_ref, ya_ref, ol_ref, yc_ref, wg_ref, woa_ref, wuv_ref, wob_ref, woc_ref, wout_ref,
                  g_ref, b_ref, o_ref):
    h = h_ref[...]
    gate = _sigmoid(_dot(h.astype(BF16), wg_ref[...]))
    ya = _dot(ya_ref[...].astype(BF16), woa_ref[...])
    ob = _dot(ol_ref[...].astype(BF16), wuv_ref[...])
    yb = _dot(ob.astype(BF16), wob_ref[...])
    yc = _dot(yc_ref[...].astype(BF16), woc_ref[...])
    d = D_MODEL
    merged = gate[:, :d] * ya + gate[:, d:2 * d] * yb + gate[:, 2 * d:] * yc
    mix = _dot(merged.astype(BF16), wout_ref[...])
    o_ref[...] = _ln_rows(DN_ALPHA * h + mix, g_ref[...], b_ref[...], LN_EPS)


def _merge(h, ya, olat, yc, lw):
    n, d = h.shape
    tm = _pow2_tile(n, 256)
    row = lambda w: pl.BlockSpec((tm, w), lambda i: (i, 0))
    ws = [lw["w_gate"], lw["wo_a"], lw["w_uv_bd"], lw["wo_b"], lw["wo_c"], lw["w_out"], lw["ln1_g"], lw["ln1_b"]]
    return pl.pallas_call(
        _merge_kernel,
        grid=(n // tm,),
        in_specs=[row(d), row(A_WIDTH), row(H_B * KV_LORA), row(H_C * DV_C)] + [_const_spec(w.shape) for w in ws],
        out_specs=row(d),
        out_shape=jax.ShapeDtypeStruct((n, d), F32),
        compiler_params=_params(("parallel",)),
        name="merge_ln1",
    )(h, ya, olat, yc, *ws)


def _top2_sum(a, b, c, d):
    m_ab, n_ab = jnp.maximum(a, b), jnp.minimum(a, b)
    m_cd, n_cd = jnp.maximum(c, d), jnp.minimum(c, d)
    return jnp.maximum(m_ab, m_cd) + jnp.maximum(jnp.minimum(m_ab, m_cd), jnp.maximum(n_ab, n_cd))


def _ffn_kernel(h_ref, p_ref, rwt_ref, rb_ref, wpg_ref, wp_ref, weg_ref, weu_ref, wed_ref, wsg_ref, wsu_ref,
                wsd_ref, exp_ref, g_ref, b_ref, o_ref):
    h = h_ref[...]
    hb = h.astype(BF16)
    tm = h.shape[0]
    ple = _sigmoid(_dot(hb, wpg_ref[...])) * _dot(p_ref[...].astype(BF16), wp_ref[...])

    logits = lax.dot_general(rwt_ref[...], h, (((1,), (1,)), ((), ())), precision=lax.Precision.HIGHEST,
                             preferred_element_type=F32)
    score = _sigmoid(logits)
    sel = score + rb_ref[...]
    srow = [sel[e:e + 1] for e in range(N_EXPERTS)]
    grp_score = [_top2_sum(*srow[E_PER_GROUP * g:E_PER_GROUP * (g + 1)]) for g in range(N_GROUPS)]
    best = grp_score[0]
    gidx = jnp.zeros((1, tm), jnp.int32)
    for g in range(1, N_GROUPS):
        upd = grp_score[g] > best
        gidx = jnp.where(upd, g, gidx)
        best = jnp.where(upd, grp_score[g], best)
    wrow = []
    for e in range(N_EXPERTS):
        g = e // E_PER_GROUP
        rank = jnp.zeros((1, tm), jnp.int32)
        for e2 in range(E_PER_GROUP * g, E_PER_GROUP * (g + 1)):
            if e2 < e:
                rank += (srow[e2] >= srow[e]).astype(jnp.int32)
            elif e2 > e:
                rank += (srow[e2] > srow[e]).astype(jnp.int32)
        chosen = (gidx == g) & (rank < 2)
        wrow.append(jnp.where(chosen, score[e:e + 1], 0.0))
    wsum = wrow[0]
    for e in range(1, N_EXPERTS):
        wsum = wsum + wrow[e]
    gates = jnp.concatenate(wrow, axis=0) / wsum
    gt = gates.T
    g_hi, g_lo = _split_bf16(gt)
    cw = E_PER_GROUP * D_FF_E

    def group_ffn(x, w_hi, w_lo, g):
        cs = slice(g * cw, (g + 1) * cw)
        hg = _dot(x, weg_ref[:, cs])
        hu = _dot(x, weu_ref[:, cs])
        gfull = _dot(w_hi, exp_ref[:, cs]) + _dot(w_lo, exp_ref[:, cs])
        act = (hg * _sigmoid(hg)) * hu * gfull
        return _dot(act.astype(BF16), wed_ref[cs, :])

    def dense_routed():
        routed = jnp.zeros((tm, D_MODEL), F32)
        for g in range(N_GROUPS):
            routed = routed + group_ffn(hb, g_hi, g_lo, g)
        return routed

    slot = -(-(tm // N_GROUPS) * MOE_SLOT_NUM // (8 * 16)) * 16
    member =jnp.concatenate([(gidx == g).astype(F32) for g in range(N_GROUPS)], axis=0)
    earlier = (lax.broadcasted_iota(jnp.int32, (tm, tm), 0)
               < lax.broadcasted_iota(jnp.int32, (tm, tm), 1)).astype(BF16)
    before = _dot(member.astype(BF16), earlier)
    rank = jnp.sum(member * before, axis=0, keepdims=True)
    dest = gidx.astype(F32) * slot + rank
    perm = (lax.broadcasted_iota(jnp.int32, (N_GROUPS * slot, tm), 0).astype(F32) == dest).astype(BF16)

    def sorted_routed():
        hs = _dot(perm, hb).astype(BF16)
        ws_hi, ws_lo = _split_bf16(_dot(perm, g_hi) + _dot(perm, g_lo))
        outs = [group_ffn(hs[g * slot:(g + 1) * slot], ws_hi[g * slot:(g + 1) * slot],
                          ws_lo[g * slot:(g + 1) * slot], g) for g in range(N_GROUPS)]
        o_hi, o_lo = _split_bf16(jnp.concatenate(outs, axis=0))
        return _dot_tn(perm, o_hi) + _dot_tn(perm, o_lo)

    if tm > slot:
        routed = lax.cond(jnp.max(rank) >= slot, dense_routed, sorted_routed)
    else:
        routed = dense_routed()
    sg = _dot(hb, wsg_ref[...])
    shared = _dot(((sg * _sigmoid(sg)) * _dot(hb, wsu_ref[...])).astype(BF16), wsd_ref[...])
    o_ref[...] = _ln_rows(DN_ALPHA * h + (routed + shared) + ple, g_ref[...], b_ref[...], LN_EPS)


def _ffn(h, p, lw, shared):
    n, d = h.shape
    tm = _pow2_tile(n, 512)
    row = lambda w: pl.BlockSpec((tm, w), lambda i: (i, 0))
    ws = [shared["router_wt"], shared["router_b"], lw["w_ple_gate"], lw["w_ple"], lw["we_gate"], lw["we_up"],
          lw["we_down"], lw["ws_gate"], lw["ws_up"], lw["ws_down"], shared["expand"], lw["ln2_g"], lw["ln2_b"]]
    return pl.pallas_call(
        _ffn_kernel,
        grid=(n // tm,),
        in_specs=[row(d), row(PLE_DIM)] + [_const_spec(w.shape) for w in ws],
        out_specs=row(d),
        out_shape=jax.ShapeDtypeStruct((n, d), F32),
        compiler_params=_params(("parallel",)),
        name="ple_moe_ln2",
    )(h, p, *ws)


def _block_diag(blocks):
    nb = len(blocks)
    r, c = blocks[0].shape
    out = jnp.zeros((nb * r, nb * c), blocks[0].dtype)
    for i, b in enumerate(blocks):
        out = out.at[i * r:(i + 1) * r, i * c:(i + 1) * c].set(b)
    return out


def _layer_weights(i, w_in, mu_a, w0, w2, a0, a2, g2, k_k, k_a, r_k, lnx_g, lnx_b, wo_a, q_norm_g, w_qb, kv_norm_g,
                   w_uk, w_uv, wo_b, gn_c_g, gn_c_b, wo_c, w_out, ln1_g, ln1_b, we_gate, we_up, we_down, ws_gate,
                   ws_up, ws_down, w_ple_gate, w_ple, ln2_g, ln2_b):
    win = w_in[i]
    o0, o1, o2 = A_COLS, A_COLS + B_COLS, A_COLS + B_COLS + C_COLS
    wb = win[:, o0:o1]
    half = ROPE_B // 2
    kpe0 = Q_LORA + KV_LORA
    wb_pad = jnp.zeros((D_MODEL, B_PAD), F32).at[:, :B_COLS].set(wb)
    wb_pad = wb_pad.at[:, 512:512 + half].set(wb[:, kpe0 + half:kpe0 + ROPE_B])
    wb_pad = wb_pad.at[:, 512 + half:512 + ROPE_B].set(wb[:, kpe0:kpe0 + half])
    zeros = jnp.zeros((W_LORA, A_WIDTH), F32)
    w_wa = jnp.concatenate([jnp.concatenate([w2[i], zeros], 1), jnp.concatenate([zeros, a2[i]], 1)], 0)
    wq = w_qb[i].reshape(Q_LORA, H_B, NOPE_B + ROPE_B)
    wq_pe = wq[:, :, NOPE_B:].transpose(1, 0, 2)
    wq_pe_sw = jnp.concatenate([wq_pe[..., half:], wq_pe[..., :half]], -1)
    vec = lambda x: x.reshape(1, -1).astype(F32)
    return dict(
        w_a=win[:, :o0].astype(BF16), w_b=wb_pad.astype(BF16), w_c=win[:, o1:o2].astype(BF16),
        w_gate=win[:, o2:].astype(BF16),
        mu_a=vec(mu_a[i]), w0=vec(w0[i]), a0=vec(a0[i]), w_wa=w_wa.astype(BF16), g2=g2[i].astype(BF16),
        k_k=vec(k_k[i]), k_a=vec(k_a[i]), r_k=vec(r_k[i]), lnx_g=vec(lnx_g[i]), lnx_b=vec(lnx_b[i]),
        wo_a=wo_a[i].astype(BF16),
        q_norm_g=vec(q_norm_g[i]), kv_norm_g=vec(kv_norm_g[i]),
        w_qn=wq[:, :, :NOPE_B].reshape(Q_LORA, H_B * NOPE_B).astype(BF16),
        w_qpe=wq_pe.astype(BF16), w_qpe_sw=wq_pe_sw.astype(BF16),
        w_uk_bd=_block_diag([w_uk[i][:, h, :].T for h in range(H_B)]).astype(BF16),
        w_uv_bd=_block_diag([w_uv[i][:, h, :] for h in range(H_B)]).astype(BF16),
        wo_b=wo_b[i].astype(BF16),
        gn_c_g=vec(gn_c_g[i]), gn_c_b=vec(gn_c_b[i]), wo_c=wo_c[i].astype(BF16),
        w_out=w_out[i].astype(BF16), ln1_g=vec(ln1_g[i]), ln1_b=vec(ln1_b[i]),
        we_gate=we_gate[i].transpose(1, 0, 2).reshape(D_MODEL, N_EXPERTS * D_FF_E).astype(BF16),
        we_up=we_up[i].transpose(1, 0, 2).reshape(D_MODEL, N_EXPERTS * D_FF_E).astype(BF16),
        we_down=we_down[i].reshape(N_EXPERTS * D_FF_E, D_MODEL).astype(BF16),
        ws_gate=ws_gate[i].astype(BF16), ws_up=ws_up[i].astype(BF16), ws_down=ws_down[i].astype(BF16),
        w_ple_gate=w_ple_gate[i].astype(BF16), w_ple=w_ple[i].astype(BF16),
        ln2_g=vec(ln2_g[i]), ln2_b=vec(ln2_b[i]),
    )


def _rope_tables(pos, half):
    inv = ROPE_THETA ** (-jnp.arange(half, dtype=F32) / half)
    ang = pos.astype(F32)[:, None] * inv[None, :]
    cos, sin = jnp.cos(ang), jnp.sin(ang)
    return jnp.concatenate([cos, cos], -1), jnp.concatenate([-sin, sin], -1)


def _wkv_to_lanes(s):
    B = s.shape[0]
    return s.transpose(0, 2, 1, 3).reshape(B, N_A, A_WIDTH)


def _wkv_from_lanes(s):
    B = s.shape[0]
    return s.reshape(B, N_A, H_A, N_A).transpose(0, 2, 1, 3)


def _trunk_layer(h, p, tabs, shift0, wkv0, ret0, attend, lw, shared, B, T, wkv_blk, ret_bb, ret_layer=None):
    pa, pb, pc = _proj(h, lw["w_a"], lw["w_b"], lw["w_c"])
    ya, wkv = _wkv(pa.reshape(B, T, A_COLS), shift0, wkv0, lw, wkv_blk[0], wkv_blk[1], F32)
    qcat, kv_row, kv_bf = _mla_pre(pb, tabs["cos_b"], tabs["sin_b"], lw)
    olat = attend(qcat, kv_row, kv_bf)
    yc, ret = _retention(pc, tabs["cos_c"], tabs["sin_c"], ret0, lw, B, T, ret_bb, ret_layer)
    h1 = _merge(h, ya.reshape(B * T, A_WIDTH), olat, yc, lw)
    h2 = _ffn(h1, p, lw, shared)
    shift = pa.reshape(B, T, A_COLS)[:, -1]
    return h2, kv_row.reshape(B, T, KV_DIM), wkv, shift, ret


def kernel(x_prompt, x_sample, cache_mla, state_wkv, state_shift, state_ret, page_table, p_prompt, p_sample,
           ln_emb_g, ln_emb_b, w_in, mu_a, w0, w2, a0, a2, g2, k_k, k_a, r_k, lnx_g, lnx_b, wo_a,
           q_norm_g, w_qb, kv_norm_g, w_uk, w_uv, wo_b, gn_c_g, gn_c_b, wo_c, w_out, ln1_g, ln1_b,
           router_w, router_b, we_gate, we_up, we_down, ws_gate, ws_up, ws_down, w_ple_gate, w_ple, ln2_g, ln2_b):
    Bp, Sp, D = x_prompt.shape
    Bs, Ts, _ = x_sample.shape
    past_len = page_table.shape[1] * PAGE_SIZE

    pos_p = jnp.tile(jnp.arange(Sp, dtype=jnp.int32), Bp)
    pos_s = jnp.tile(past_len + jnp.arange(Ts, dtype=jnp.int32), Bs)
    tabs = []
    for pos in (pos_p, pos_s):
        cb, sb = _rope_tables(pos, ROPE_B // 2)
        cc, sc = _rope_tables(pos, DK_C // 2)
        tabs.append(dict(cos_b=cb, sin_b=sb, cos_c=cc, sin_c=sc))
    expand = (jnp.arange(N_EXPERTS)[:, None] == (jnp.arange(N_EXPERTS * D_FF_E)[None, :] // D_FF_E)).astype(BF16)
    shared = dict(router_wt=router_w.T.astype(F32), router_b=router_b.reshape(N_EXPERTS, 1).astype(F32),
                  expand=expand)

    cache_t = jnp.swapaxes(cache_mla, 2, 3)

    hp = _layer_norm(x_prompt.reshape(Bp * Sp, D), ln_emb_g, ln_emb_b)
    hs = _layer_norm(x_sample.reshape(Bs * Ts, D), ln_emb_g, ln_emb_b)
    zero_shift = jnp.zeros((Bp, A_COLS), F32)
    zero_wkv = jnp.zeros((Bp, N_A, A_WIDTH), F32)
    zero_ret = jnp.zeros((Bp, H_C, DK_C, DV_C), F32)
    wkv_tc = min(64, Sp)
    samp_bb = 8 if Bs % 8 == 0 else Bs

    outs = [[] for _ in range(8)]
    for i in range(DEPTH):
        lw = _layer_weights(i, w_in, mu_a, w0, w2, a0, a2, g2, k_k, k_a, r_k, lnx_g, lnx_b, wo_a, q_norm_g, w_qb,
                            kv_norm_g, w_uk, w_uv, wo_b, gn_c_g, gn_c_b, wo_c, w_out, ln1_g, ln1_b, we_gate, we_up,
                            we_down, ws_gate, ws_up, ws_down, w_ple_gate, w_ple, ln2_g, ln2_b)
        attend_p = lambda qcat, kv_row, kv_bf: _flash_prompt(qcat, kv_bf, Bp, Sp)
        hp, kv_p, wkv_p, sh_p, ret_p = _trunk_layer(
            hp, p_prompt[i].reshape(Bp * Sp, PLE_DIM), tabs[0], zero_shift, zero_wkv, zero_ret, attend_p, lw, shared,
            Bp, Sp, (Bp, wkv_tc), 1)
        attend_s = lambda qcat, kv_row, kv_bf, i=i: _sample_attn(qcat.astype(F32), kv_row, cache_t, page_table, i,
                                                                 Bs, Ts)
        hs, kv_s, wkv_s, sh_s, ret_s = _trunk_layer(
            hs, p_sample[i].reshape(Bs * Ts, PLE_DIM), tabs[1], state_shift[i], _wkv_to_lanes(state_wkv[i]),
            state_ret, attend_s, lw, shared, Bs, Ts, (samp_bb, Ts), samp_bb, ret_layer=i)
        for lst, val in zip(outs, (kv_p, kv_s, _wkv_from_lanes(wkv_p), _wkv_from_lanes(wkv_s), sh_p, sh_s,
                                   ret_p, ret_s)):
            lst.append(val)
    return (hp.reshape(Bp, Sp, D), hs.reshape(Bs, Ts, D)) + tuple(jnp.stack(l) for l in outs)
```
